```python
import math
import jax, jax.numpy as jnp
from jax import lax
import numpy as np

D_MODEL = 1024
BATCH = 8
SEQ = 2048
DEPTH = 4

N_MIXERS = 2
N_MEM = 256
XA_HEADS = 4
XA_WIDTH = D_MODEL // 4
XA_HEAD_DIM = XA_WIDTH // XA_HEADS
MIX_WIDTH = D_MODEL - XA_WIDTH
GLA_HEADS = 4
GLA_DV = MIX_WIDTH
GLA_DK = MIX_WIDTH // 2
GLA_HEAD_K = GLA_DK // GLA_HEADS
GLA_HEAD_V = GLA_DV // GLA_HEADS
GLA_GATE_RANK = 16
GLA_GATE_TAU = 16.0
GLA_CHUNK = 64
CONV_GROUPS = 4
CONV_WIDTH = 3
D_FF = int(math.ceil(8 * D_MODEL / 3 / 128)) * 128
FFN_CONV_WIDTH = 3
LN_EPS = 1e-5
DEEPNORM_ALPHA = (2 * DEPTH) ** 0.25
DEEPNORM_BETA = (8 * DEPTH) ** -0.25
N_GLA_LAYERS = (DEPTH + 1) // 2
N_CONV_LAYERS = DEPTH // 2
GLA_IN_SIZES = (GLA_DK, GLA_DK, GLA_DV, GLA_DV, GLA_GATE_RANK, XA_WIDTH)
CONV_IN_SIZES = (MIX_WIDTH, MIX_WIDTH, MIX_WIDTH, XA_WIDTH)
GLA_IN = sum(GLA_IN_SIZES)
CONV_IN = sum(CONV_IN_SIZES)

kernel_name = "hybrid_gla_shortconv_memxattn"


def split_cols(t, sizes):
    idx = np.cumsum(np.array(sizes))[:-1].tolist()
    return jnp.split(t, idx, axis=-1)


def layer_norm(x, g, b):
    xf = x.astype(jnp.float32)
    mu = jnp.mean(xf, axis=-1, keepdims=True)
    var = jnp.mean(jnp.square(xf - mu), axis=-1, keepdims=True)
    return ((xf - mu) * lax.rsqrt(var + LN_EPS)).astype(x.dtype) * g + b


def causal_dwconv(x, w):
    K = w.shape[0]
    S = x.shape[1]
    xp = jnp.pad(x, ((0, 0), (K - 1, 0), (0, 0)))
    y = xp[:, 0:S, :] * w[0]
    for kk in range(1, K):
        y = y + xp[:, kk:kk + S, :] * w[kk]
    return y


def gla_mixer(q, k, v, r, gate_lr, w_a2, b_a, head_g):
    bsz, seq, _ = q.shape
    n_chunks = seq // GLA_CHUNK
    f32 = jnp.float32

    def to_chunks(t, dh):
        return t.reshape(bsz, n_chunks, GLA_CHUNK, GLA_HEADS, dh).transpose(0, 3, 1, 2, 4)

    log_a = jax.nn.log_sigmoid((gate_lr @ w_a2 + b_a).astype(f32)) / GLA_GATE_TAU
    G = jnp.cumsum(to_chunks(log_a, GLA_HEAD_K), axis=3)
    G_last = G[:, :, :, -1:, :]
    qc = to_chunks(q, GLA_HEAD_K).astype(f32) * (GLA_HEAD_K ** -0.5)
    kc = to_chunks(k, GLA_HEAD_K).astype(f32)
    vc = to_chunks(v, GLA_HEAD_V).astype(f32)

    q_dec = qc * jnp.exp(G)
    k_inv = kc * jnp.exp(-G)
    k_end = kc * jnp.exp(G_last - G)

    causal = jnp.tril(jnp.ones((GLA_CHUNK, GLA_CHUNK), dtype=bool))
    att = jnp.einsum('bhnik,bhnjk->bhnij', q_dec, k_inv)
    att = jnp.where(causal, att, 0.0)
    o_intra = jnp.einsum('bhnij,bhnjv->bhniv', att, vc)

    kv_chunk = jnp.einsum('bhnck,bhncv->bhnkv', k_end, vc)
    decay_chunk = jnp.exp(G_last[:, :, :, 0, :])

    def step(state, inp):
        kv_n, d_n = inp
        return d_n[..., None] * state + kv_n, state

    init = jnp.zeros((bsz, GLA_HEADS, GLA_HEAD_K, GLA_HEAD_V), f32)
    _, states = lax.scan(step, init, (jnp.moveaxis(kv_chunk, 2, 0), jnp.moveaxis(decay_chunk, 2, 0)))
    states = jnp.moveaxis(states, 0, 2)
    o_inter = jnp.einsum('bhnck,bhnkv->bhncv', q_dec, states)

    o = o_intra + o_inter
    o = o * lax.rsqrt(jnp.mean(jnp.square(o), axis=-1, keepdims=True) + LN_EPS)
    o = o.transpose(0, 2, 3, 1, 4).reshape(bsz, seq, GLA_DV).astype(v.dtype) * head_g
    return o * jax.nn.silu(r)


def short_conv_mixer(b_gate, c_gate, h, conv_w):
    return b_gate * causal_dwconv(c_gate * h, conv_w)


def memory_xattn(q, mem_kv):
    bsz, seq, _ = q.shape
    m = mem_kv.shape[1]
    k, v = jnp.split(mem_kv, 2, axis=-1)
    qh = q.reshape(bsz, seq, XA_HEADS, XA_HEAD_DIM)
    kh = k.reshape(bsz, m, XA_HEADS, XA_HEAD_DIM)
    vh = v.reshape(bsz, m, XA_HEADS, XA_HEAD_DIM)
    s = jnp.einsum('bshd,bmhd->bhsm', qh, kh).astype(jnp.float32) * (XA_HEAD_DIM ** -0.5)
    p = jax.nn.softmax(s, axis=-1).astype(v.dtype)
    o = jnp.einsum('bhsm,bmhd->bshd', p, vh)
    return o.reshape(bsz, seq, XA_WIDTH)


def conv_ffn(x, w_up, conv_w, conv_b, w_down):
    u = causal_dwconv(x @ w_up, conv_w) + conv_b
    gate, val = jnp.split(u, 2, axis=-1)
    return (jax.nn.silu(gate) * val) @ w_down


def setup_inputs(seed: int = 0) -> dict:
    key = jax.random.key(seed)
    ks = jax.random.split(key, 24)

    def nrm(k, shape, scale):
        return jax.random.normal(k, shape, jnp.float32) * scale

    d_inv = D_MODEL ** -0.5
    return {
        "x": nrm(ks[0], (BATCH, SEQ, D_MODEL), 1.0),
        "mem": nrm(ks[1], (BATCH, N_MEM, D_MODEL), 1.0),
        "gla_w_in": nrm(ks[2], (N_GLA_LAYERS, D_MODEL, GLA_IN), d_inv),
        "gla_w_a2": nrm(ks[3], (N_GLA_LAYERS, GLA_GATE_RANK, GLA_DK), GLA_GATE_RANK ** -0.5),
        "gla_b_a": nrm(ks[4], (N_GLA_LAYERS, GLA_DK), 0.01),
        "gla_head_g": 1.0 + nrm(ks[5], (N_GLA_LAYERS, GLA_DV), 0.02),
        "gla_w_out": nrm(ks[6], (N_GLA_LAYERS, D_MODEL, D_MODEL), d_inv * DEEPNORM_BETA),
        "conv_w_in": nrm(ks[7], (N_CONV_LAYERS, D_MODEL, CONV_IN), d_inv),
        "conv_w": nrm(ks[8], (N_CONV_LAYERS, CONV_WIDTH, MIX_WIDTH), CONV_WIDTH ** -0.5),
        "conv_w_out": nrm(ks[9], (N_CONV_LAYERS, D_MODEL, D_MODEL), d_inv * DEEPNORM_BETA),
        "w_mem_kv": nrm(ks[10], (DEPTH, D_MODEL, 2 * XA_WIDTH), d_inv),
        "ln1_g": 1.0 + nrm(ks[11], (DEPTH, D_MODEL), 0.02),
        "ln1_b": nrm(ks[12], (DEPTH, D_MODEL), 0.02),
        "ffn_w_up": nrm(ks[13], (DEPTH, D_MODEL, 2 * D_FF), d_inv),
        "ffn_conv_w": nrm(ks[14], (DEPTH, FFN_CONV_WIDTH, 2 * D_FF), FFN_CONV_WIDTH ** -0.5),
        "ffn_conv_b": nrm(ks[15], (DEPTH, 2 * D_FF), 0.01),
        "ffn_w_down": nrm(ks[16], (DEPTH, D_FF, D_MODEL), (D_FF ** -0.5) * DEEPNORM_BETA),
        "ln2_g": 1.0 + nrm(ks[17], (DEPTH, D_MODEL), 0.02),
        "ln2_b": nrm(ks[18], (DEPTH, D_MODEL), 0.02),
    }


def reference(x, mem, gla_w_in, gla_w_a2, gla_b_a, gla_head_g, gla_w_out,
              conv_w_in, conv_w, conv_w_out, w_mem_kv, ln1_g, ln1_b,
              ffn_w_up, ffn_conv_w, ffn_conv_b, ffn_w_down, ln2_g, ln2_b):
    for i in range(DEPTH):
        j = i // N_MIXERS
        mem_kv = mem @ w_mem_kv[i]
        if i % N_MIXERS == 0:
            proj = x @ gla_w_in[j]
            q, k, v, r, gate_lr, mem_q = split_cols(proj, GLA_IN_SIZES)
            mix = gla_mixer(q, k, v, r, gate_lr, gla_w_a2[j], gla_b_a[j], gla_head_g[j])
            w_out = gla_w_out[j]
        else:
            proj = x @ conv_w_in[j]
            b_gate, c_gate, h, mem_q = split_cols(proj, CONV_IN_SIZES)
            mix = short_conv_mixer(b_gate, c_gate, h, conv_w[j])
            w_out = conv_w_out[j]
        xa = memory_xattn(mem_q, mem_kv)
        y = jnp.concatenate([mix, xa], axis=-1) @ w_out
        x = layer_norm(DEEPNORM_ALPHA * x + y, ln1_g[i], ln1_b[i])
        f = conv_ffn(x, ffn_w_up[i], ffn_conv_w[i], ffn_conv_b[i], ffn_w_down[i])
        x = layer_norm(DEEPNORM_ALPHA * x + f, ln2_g[i], ln2_b[i])
    return x
```

```python
import functools
import math

import jax
import jax.numpy as jnp
from jax import lax
from jax.experimental import pallas as pl
from jax.experimental.pallas import tpu as pltpu

D_MODEL = 1024
DEPTH = 4
N_MEM = 256
XA_HEADS = 4
XA_WIDTH = D_MODEL // 4
XA_HEAD_DIM = XA_WIDTH // XA_HEADS
MIX_WIDTH = D_MODEL - XA_WIDTH
GLA_HEADS = 4
GLA_DV = MIX_WIDTH
GLA_DK = MIX_WIDTH // 2
GLA_HEAD_K = GLA_DK // GLA_HEADS
GLA_HEAD_V = GLA_DV // GLA_HEADS
GLA_GATE_RANK = 16
GLA_GATE_TAU = 16.0
GLA_CHUNK = 64
CONV_WIDTH = 3
D_FF = int(math.ceil(8 * D_MODEL / 3 / 128)) * 128
LN_EPS = 1e-5
DEEPNORM_ALPHA = (2 * DEPTH) ** 0.25

HEAD_K_PAD = 128
HEAD_V_PAD = 256
GLA_DK_PAD = GLA_HEADS * HEAD_K_PAD
GLA_DV_PAD = GLA_HEADS * HEAD_V_PAD
GATE_RANK_PAD = 128

HALO = 8
MIX_TILE = 256
FFN_TILE = 512
FFN_CHUNK = 256
VMEM_LIMIT_BYTES = 56 * 1024 * 1024

F32 = jnp.float32
BF16 = jnp.bfloat16
_NT = (((1,), (1,)), ((), ()))
_TN = (((0,), (0,)), ((), ()))


def _dot(a, b):
    return jnp.dot(a, b, preferred_element_type=F32)


def _layer_norm_rows(z, g, b):
    mu = jnp.mean(z, axis=-1, keepdims=True)
    zc = z - mu
    var = jnp.mean(zc * zc, axis=-1, keepdims=True)
    return zc * lax.rsqrt(var + LN_EPS) * g + b


def _build_memory_kv(mem_ref, wkt_ref, wv_ref, kbd_ref, vbd_ref):
    memb = mem_ref[...].astype(BF16)
    k_t = lax.dot_general(wkt_ref[...], memb, _NT, preferred_element_type=F32)
    v = _dot(memb, wv_ref[...])
    row_head = lax.broadcasted_iota(jnp.int32, (XA_WIDTH, N_MEM), 0) // XA_HEAD_DIM
    col_head = lax.broadcasted_iota(jnp.int32, (N_MEM, XA_WIDTH), 1) // XA_HEAD_DIM
    for h in range(XA_HEADS):
        kbd_ref[:, h * N_MEM:(h + 1) * N_MEM] = jnp.where(row_head == h, k_t, 0.0).astype(BF16)
        vbd_ref[h * N_MEM:(h + 1) * N_MEM, :] = jnp.where(col_head == h, v, 0.0).astype(BF16)


def _memory_xattn(mem_q, kbd_ref, vbd_ref):
    q = (mem_q * (XA_HEAD_DIM ** -0.5)).astype(BF16)
    s = _dot(q, kbd_ref[...])
    probs = []
    for h in range(XA_HEADS):
        sh = s[:, h * N_MEM:(h + 1) * N_MEM]
        e = jnp.exp(sh - jnp.max(sh, axis=-1, keepdims=True))
        inv = 1.0 / jnp.sum(e, axis=-1, keepdims=True)
        probs.append((e * inv).astype(BF16))
    return _dot(jnp.concatenate(probs, axis=-1), vbd_ref[...])


def _gla_layer_kernel(x_ref, mem_ref, wq_ref, wk_ref, wv_ref, wr_ref, wg_ref, wmq_ref, wa2_ref,
                      ba_ref, hg_ref, wkt_ref, wvm_ref, wo_mix_ref, wo_xa_ref, lng_ref, lnb_ref,
                      out_ref,
                      kbd_ref, vbd_ref, state_ref, q_s, k_s, v_s, r_s, la_s, o_s):
    tile = x_ref.shape[0]
    n_chunks = tile // GLA_CHUNK

    @pl.when(pl.program_id(1) == 0)
    def _():
        _build_memory_kv(mem_ref, wkt_ref, wvm_ref, kbd_ref, vbd_ref)
        state_ref[...] = jnp.zeros_like(state_ref)

    x = x_ref[...]
    xb = x.astype(BF16)
    q_s[...] = _dot(xb, wq_ref[...]) * (GLA_HEAD_K ** -0.5)
    k_s[...] = _dot(xb, wk_ref[...])
    v_s[...] = _dot(xb, wv_ref[...]).astype(BF16)
    r = _dot(xb, wr_ref[...])
    r_s[...] = r * jax.nn.sigmoid(r)
    gate_lr = _dot(xb, wg_ref[...]).astype(BF16)
    z = _dot(gate_lr, wa2_ref[...]) + ba_ref[...]
    la_s[...] = jax.nn.log_sigmoid(z) * (1.0 / GLA_GATE_TAU)

    rr = lax.broadcasted_iota(jnp.int32, (GLA_CHUNK, GLA_CHUNK), 0)
    cc = lax.broadcasted_iota(jnp.int32, (GLA_CHUNK, GLA_CHUNK), 1)
    causal = rr >= cc
    tril_ones = causal.astype(BF16)

    def chunk_body(c, carry):
        rows = pl.ds(pl.multiple_of(c * GLA_CHUNK, GLA_CHUNK), GLA_CHUNK)
        la = la_s[rows, :]
        la1 = la.astype(BF16)
        rem = la - la1.astype(F32)
        la2 = rem.astype(BF16)
        la3 = (rem - la2.astype(F32)).astype(BF16)
        g_cum = _dot(tril_ones, la1) + _dot(tril_ones, la2) + _dot(tril_ones, la3)
        g_last = g_cum[GLA_CHUNK - 1:GLA_CHUNK, :]
        kc = k_s[rows, :]
        q_dec = (q_s[rows, :] * jnp.exp(g_cum)).astype(BF16)
        k_inv = (kc * jnp.exp(-g_cum)).astype(BF16)
        k_end = (kc * jnp.exp(g_last - g_cum)).astype(BF16)
        decay = jnp.exp(g_last)
        for h in range(GLA_HEADS):
            ks = slice(h * HEAD_K_PAD, (h + 1) * HEAD_K_PAD)
            vs = slice(h * HEAD_V_PAD, (h + 1) * HEAD_V_PAD)
            qh = q_dec[:, ks]
            vh = v_s[rows, vs]
            att = lax.dot_general(qh, k_inv[:, ks], _NT, preferred_element_type=F32)
            att = jnp.where(causal, att, 0.0).astype(BF16)
            st = state_ref[h]
            o = _dot(att, vh) + lax.dot_general(qh, st.astype(BF16), _NT, preferred_element_type=F32)
            kv_t = lax.dot_general(vh, k_end[:, ks], _TN, preferred_element_type=F32)
            state_ref[h] = st * decay[:, ks] + kv_t
            ms = jnp.sum(o * o, axis=-1, keepdims=True) * (1.0 / GLA_HEAD_V)
            o = o * lax.rsqrt(ms + LN_EPS) * hg_ref[:, vs]
            o_s[rows, vs] = (o * r_s[rows, vs]).astype(BF16)
        return carry

    lax.fori_loop(0, n_chunks, chunk_body, 0)

    xa = _memory_xattn(_dot(xb, wmq_ref[...]), kbd_ref, vbd_ref)
    y = _dot(o_s[...], wo_mix_ref[...]) + _dot(xa.astype(BF16), wo_xa_ref[...])
    out_ref[...] = _layer_norm_rows(DEEPNORM_ALPHA * x + y, lng_ref[...], lnb_ref[...])


def _conv_layer_kernel(x_ref, mem_ref, wb_ref, wc_ref, wh_ref, wmq_ref, cw_ref, wkt_ref, wvm_ref,
                       wo_mix_ref, wo_xa_ref, lng_ref, lnb_ref,
                       out_ref,
                       kbd_ref, vbd_ref, ch_s):
    tile = x_ref.shape[0]

    @pl.when(pl.program_id(1) == 0)
    def _():
        _build_memory_kv(mem_ref, wkt_ref, wvm_ref, kbd_ref, vbd_ref)
        ch_s[0:HALO, :] = jnp.zeros((HALO, MIX_WIDTH), F32)

    x = x_ref[...]
    xb = x.astype(BF16)
    ch = _dot(xb, wc_ref[...]) * _dot(xb, wh_ref[...])
    ch_s[HALO:HALO + tile, :] = ch
    conv = (ch_s[pl.ds(HALO - 2, tile), :] * cw_ref[0:1, :]
            + ch_s[pl.ds(HALO - 1, tile), :] * cw_ref[1:2, :]
            + ch * cw_ref[2:3, :])
    ch_s[0:HALO, :] = ch[tile - HALO:tile, :]
    mix = (_dot(xb, wb_ref[...]) * conv).astype(BF16)

    xa = _memory_xattn(_dot(xb, wmq_ref[...]), kbd_ref, vbd_ref)
    y = _dot(mix, wo_mix_ref[...]) + _dot(xa.astype(BF16), wo_xa_ref[...])
    out_ref[...] = _layer_norm_rows(DEEPNORM_ALPHA * x + y, lng_ref[...], lnb_ref[...])


def _ffn_kernel(x_ref, wup_ref, cw_ref, cb_ref, wdn_ref, lng_ref, lnb_ref,
                out_ref,
                tail_ref, u_s, h_s):
    tile = x_ref.shape[0]

    @pl.when(pl.program_id(1) == 0)
    def _():
        tail_ref[...] = jnp.zeros_like(tail_ref)

    x = x_ref[...]
    xb = x.astype(BF16)

    def conv_cols(half, j):
        cols = slice(half * D_FF + j * FFN_CHUNK, half * D_FF + (j + 1) * FFN_CHUNK)
        u = _dot(xb, wup_ref[:, cols])
        u_s[half, 0:HALO, :] = tail_ref[:, cols]
        u_s[half, HALO:HALO + tile, :] = u
        tail_ref[:, cols] = u[tile - HALO:tile, :]
        return (u_s[half, pl.ds(HALO - 2, tile), :] * cw_ref[0:1, cols]
                + u_s[half, pl.ds(HALO - 1, tile), :] * cw_ref[1:2, cols]
                + u * cw_ref[2:3, cols] + cb_ref[:, cols])

    for j in range(D_FF // FFN_CHUNK):
        gate = conv_cols(0, j)
        val = conv_cols(1, j)
        h_s[:, j * FFN_CHUNK:(j + 1) * FFN_CHUNK] = (gate * jax.nn.sigmoid(gate) * val).astype(BF16)

    y = _dot(h_s[...], wdn_ref[...])
    out_ref[...] = _layer_norm_rows(DEEPNORM_ALPHA * x + y, lng_ref[...], lnb_ref[...])


def _full_spec(arr):
    zeros = (0,) * arr.ndim
    return pl.BlockSpec(arr.shape, lambda b, s: zeros, pipeline_mode=pl.Buffered(1))


def _row_spec(tile, width):
    return pl.BlockSpec((None, tile, width), lambda b, s: (b, s, 0))


def _mem_spec():
    return pl.BlockSpec((None, N_MEM, D_MODEL), lambda b, s: (b, 0, 0))


def _compiler_params():
    return pltpu.CompilerParams(dimension_semantics=("arbitrary", "arbitrary"),
                                vmem_limit_bytes=VMEM_LIMIT_BYTES)


def _gla_layer(x, mem, weights):
    bsz, seq, _ = x.shape
    tile = min(MIX_TILE, seq)
    return pl.pallas_call(
        _gla_layer_kernel,
        grid=(bsz, seq // tile),
        in_specs=[_row_spec(tile, D_MODEL), _mem_spec()] + [_full_spec(w) for w in weights],
        out_specs=_row_spec(tile, D_MODEL),
        out_shape=jax.ShapeDtypeStruct(x.shape, F32),
        scratch_shapes=[
            pltpu.VMEM((XA_WIDTH, XA_HEADS * N_MEM), BF16),
            pltpu.VMEM((XA_HEADS * N_MEM, XA_WIDTH), BF16),
            pltpu.VMEM((GLA_HEADS, HEAD_V_PAD, HEAD_K_PAD), F32),
            pltpu.VMEM((tile, GLA_DK_PAD), F32),
            pltpu.VMEM((tile, GLA_DK_PAD), F32),
            pltpu.VMEM((tile, GLA_DV_PAD), BF16),
            pltpu.VMEM((tile, GLA_DV_PAD), F32),
            pltpu.VMEM((tile, GLA_DK_PAD), F32),
            pltpu.VMEM((tile, GLA_DV_PAD), BF16),
        ],
        compiler_params=_compiler_params(),
        name="gla_layer",
    )(x, mem, *weights)


def _conv_layer(x, mem, weights):
    bsz, seq, _ = x.shape
    tile = min(MIX_TILE, seq)
    return pl.pallas_call(
        _conv_layer_kernel,
        grid=(bsz, seq // tile),
        in_specs=[_row_spec(tile, D_MODEL), _mem_spec()] + [_full_spec(w) for w in weights],
        out_specs=_row_spec(tile, D_MODEL),
        out_shape=jax.ShapeDtypeStruct(x.shape, F32),
        scratch_shapes=[
            pltpu.VMEM((XA_WIDTH, XA_HEADS * N_MEM), BF16),
            pltpu.VMEM((XA_HEADS * N_MEM, XA_WIDTH), BF16),
            pltpu.VMEM((HALO + tile, MIX_WIDTH), F32),
        ],
        compiler_params=_compiler_params(),
        name="conv_layer",
    )(x, mem, *weights)


def _ffn_layer(x, weights):
    bsz, seq, _ = x.shape
    tile = min(FFN_TILE, seq)
    return pl.pallas_call(
        _ffn_kernel,
        grid=(bsz, seq // tile),
        in_specs=[_row_spec(tile, D_MODEL)] + [_full_spec(w) for w in weights],
        out_specs=_row_spec(tile, D_MODEL),
        out_shape=jax.ShapeDtypeStruct(x.shape, F32),
        scratch_shapes=[
            pltpu.VMEM((HALO, 2 * D_FF), F32),
            pltpu.VMEM((2, HALO + tile, FFN_CHUNK), F32),
            pltpu.VMEM((tile, D_FF), BF16),
        ],
        compiler_params=_compiler_params(),
        name="conv_ffn",
    )(x, *weights)


def _pad_heads(w, head_width, padded_width):
    lead = w.shape[:-1]
    w = w.reshape(lead + (GLA_HEADS, head_width))
    w = jnp.pad(w, [(0, 0)] * len(lead) + [(0, 0), (0, padded_width - head_width)])
    return w.reshape(lead + (GLA_HEADS * padded_width,))


def _row(v):
    return v.reshape(1, -1)


def _memory_weights(w_mem_kv_i):
    return w_mem_kv_i[:, :XA_WIDTH].T.astype(BF16), w_mem_kv_i[:, XA_WIDTH:].astype(BF16)


def kernel(x, mem, gla_w_in, gla_w_a2, gla_b_a, gla_head_g, gla_w_out, conv_w_in, conv_w, conv_w_out,
           w_mem_kv, ln1_g, ln1_b, ffn_w_up, ffn_conv_w, ffn_conv_b, ffn_w_down, ln2_g, ln2_b):
    for i in range(DEPTH):
        j = i // 2
        wkt, wvm = _memory_weights(w_mem_kv[i])
        if i % 2 == 0:
            w_in = gla_w_in[j]
            o_k, o_v, o_r, o_g, o_mq = GLA_DK, 2 * GLA_DK, 2 * GLA_DK + GLA_DV, 2 * GLA_DK + 2 * GLA_DV, \
                2 * GLA_DK + 2 * GLA_DV + GLA_GATE_RANK
            wq = _pad_heads(w_in[:, :o_k], GLA_HEAD_K, HEAD_K_PAD).astype(BF16)
            wk = _pad_heads(w_in[:, o_k:o_v], GLA_HEAD_K, HEAD_K_PAD).astype(BF16)
            wv = _pad_heads(w_in[:, o_v:o_r], GLA_HEAD_V, HEAD_V_PAD).astype(BF16)
            wr = _pad_heads(w_in[:, o_r:o_g], GLA_HEAD_V, HEAD_V_PAD).astype(BF16)
            wg = jnp.pad(w_in[:, o_g:o_mq], ((0, 0), (0, GATE_RANK_PAD - GLA_GATE_RANK))).astype(BF16)
            wmq = w_in[:, o_mq:].astype(BF16)
            wa2 = jnp.pad(_pad_heads(gla_w_a2[j], GLA_HEAD_K, HEAD_K_PAD),
                          ((0, GATE_RANK_PAD - GLA_GATE_RANK), (0, 0))).astype(BF16)
            ba = _row(_pad_heads(gla_b_a[j], GLA_HEAD_K, HEAD_K_PAD))
            hg = _row(_pad_heads(gla_head_g[j], GLA_HEAD_V, HEAD_V_PAD))
            w_out = gla_w_out[j]
            wo_mix = _pad_heads(w_out[:MIX_WIDTH].T, GLA_HEAD_V, HEAD_V_PAD).T.astype(BF16)
            wo_xa = w_out[MIX_WIDTH:].astype(BF16)
            x = _gla_layer(x, mem, (wq, wk, wv, wr, wg, wmq, wa2, ba, hg, wkt, wvm, wo_mix, wo_xa,
                                    _row(ln1_g[i]), _row(ln1_b[i])))
        else:
            w_in = conv_w_in[j]
            wb = w_in[:, :MIX_WIDTH].astype(BF16)
            wc = w_in[:, MIX_WIDTH:2 * MIX_WIDTH].astype(BF16)
            wh = w_in[:, 2 * MIX_WIDTH:3 * MIX_WIDTH].astype(BF16)
            wmq = w_in[:, 3 * MIX_WIDTH:].astype(BF16)
            w_out = conv_w_out[j]
            x = _conv_layer(x, mem, (wb, wc, wh, wmq, conv_w[j], wkt, wvm,
                                     w_out[:MIX_WIDTH].astype(BF16), w_out[MIX_WIDTH:].astype(BF16),
                                     _row(ln1_g[i]), _row(ln1_b[i])))
        x = _ffn_layer(x, (ffn_w_up[i].astype(BF16), ffn_conv_w[i], _row(ffn_conv_b[i]),
                           ffn_w_down[i].astype(BF16), _row(ln2_g[i]), _row(ln2_b[i])))
    return x
```

```python
import functools
import math

import jax
import jax.numpy as jnp
from jax import lax
from jax.experimental import pallas as pl
from jax.experimental.pallas import tpu as pltpu

D_MODEL = 1024
DEPTH = 4
N_MEM = 256
XA_HEADS = 4
XA_WIDTH = D_MODEL // 4
XA_HEAD_DIM = XA_WIDTH // XA_HEADS
MIX_WIDTH = D_MODEL - XA_WIDTH
GLA_HEADS = 4
GLA_DV = MIX_WIDTH
GLA_DK = MIX_WIDTH // 2
GLA_HEAD_K = GLA_DK // GLA_HEADS
GLA_HEAD_V = GLA_DV // GLA_HEADS
GLA_GATE_RANK = 16
GLA_GATE_TAU = 16.0
GLA_CHUNK = 64
CONV_WIDTH = 3
D_FF = int(math.ceil(8 * D_MODEL / 3 / 128)) * 128
LN_EPS = 1e-5
DEEPNORM_ALPHA = (2 * DEPTH) ** 0.25

HEAD_K_PAD = 128
HEAD_V_PAD = 256
GLA_DK_PAD = GLA_HEADS * HEAD_K_PAD
GLA_DV_PAD = GLA_HEADS * HEAD_V_PAD
GATE_COL = GLA_HEAD_K

HALO = 8
MIX_TILE = 512
FFN_TILE = 512
FFN_CHUNK = 256
VMEM_LIMIT_BYTES = 56 * 1024 * 1024

F32 = jnp.float32
BF16 = jnp.bfloat16
_NT = (((1,), (1,)), ((), ()))
_TN = (((0,), (0,)), ((), ()))

_dot = functools.partial(jnp.dot, preferred_element_type=F32)
_dot_general = functools.partial(lax.dot_general, preferred_element_type=F32)


def _layer_norm_rows(z, g, b):
    mu = jnp.mean(z, axis=-1, keepdims=True)
    zc = z - mu
    var = jnp.mean(zc * zc, axis=-1, keepdims=True)
    return zc * lax.rsqrt(var + LN_EPS) * g + b


def _build_memory_kv(mem_ref, wkt_ref, wv_ref, kbd_ref, vbd_ref):
    memb = mem_ref[...].astype(BF16)
    k_t = _dot_general(wkt_ref[...], memb, _NT)
    v = _dot(memb, wv_ref[...])
    row_head = lax.broadcasted_iota(jnp.int32, (XA_WIDTH, N_MEM), 0) // XA_HEAD_DIM
    col_head = lax.broadcasted_iota(jnp.int32, (N_MEM, XA_WIDTH), 1) // XA_HEAD_DIM
    for h in range(XA_HEADS):
        kbd_ref[:, h * N_MEM:(h + 1) * N_MEM] = jnp.where(row_head == h, k_t, 0.0).astype(BF16)
        vbd_ref[h * N_MEM:(h + 1) * N_MEM, 0:XA_WIDTH] = jnp.where(col_head == h, v, 0.0).astype(BF16)
        vbd_ref[h * N_MEM:(h + 1) * N_MEM, XA_WIDTH:] = jnp.where(col_head == h, 1.0, 0.0).astype(BF16)


def _memory_xattn(mem_q, kbd_ref, vbd_ref):
    q = (mem_q * (XA_HEAD_DIM ** -0.5)).astype(BF16)
    s = _dot(q, kbd_ref[...])
    es = []
    for h in range(XA_HEADS):
        sh = s[:, h * N_MEM:(h + 1) * N_MEM]
        es.append(jnp.exp(sh - jnp.max(sh, axis=-1, keepdims=True)).astype(BF16))
    acc = _dot(jnp.concatenate(es, axis=-1), vbd_ref[...])
    return acc[:, :XA_WIDTH] / acc[:, XA_WIDTH:]


def _gla_layer_kernel(x_ref, mem_ref, wq_ref, wk_ref, wv_ref, wr_ref, wmq_ref, wa2_ref,
                      ba_ref, hg_ref, wkt_ref, wvm_ref, wo_mix_ref, wo_xa_ref, lng_ref, lnb_ref,
                      out_ref,
                      kbd_ref, vbd_ref, state_ref, q_s, k_s, v_s, r_s, la_s, o_s):
    tile = x_ref.shape[0]
    n_chunks = tile // GLA_CHUNK

    @pl.when(pl.program_id(1) == 0)
    def _():
        _build_memory_kv(mem_ref, wkt_ref, wvm_ref, kbd_ref, vbd_ref)
        state_ref[...] = jnp.zeros_like(state_ref)

    x = x_ref[...]
    xb = x.astype(BF16)
    q_raw = _dot(xb, wq_ref[...])
    q_s[...] = q_raw * (GLA_HEAD_K ** -0.5)
    z = _dot(q_raw[:, 0:HEAD_K_PAD].astype(BF16), wa2_ref[...]) + ba_ref[...]
    la_s[...] = jax.nn.log_sigmoid(z) * (1.0 / GLA_GATE_TAU)
    k_s[...] = _dot(xb, wk_ref[...])
    v_s[...] = _dot(xb, wv_ref[...]).astype(BF16)
    r = _dot(xb, wr_ref[...])
    r_s[...] = r * jax.nn.sigmoid(r)

    rr = lax.broadcasted_iota(jnp.int32, (GLA_CHUNK, GLA_CHUNK), 0)
    cc = lax.broadcasted_iota(jnp.int32, (GLA_CHUNK, GLA_CHUNK), 1)
    causal = rr >= cc
    tril_ones = causal.astype(BF16)

    for c in range(n_chunks):
        rows = slice(c * GLA_CHUNK, (c + 1) * GLA_CHUNK)
        la = la_s[rows, :]
        la1 = la.astype(BF16)
        rem = la - la1.astype(F32)
        la2 = rem.astype(BF16)
        la3 = (rem - la2.astype(F32)).astype(BF16)
        g_cum = _dot(tril_ones, la1) + _dot(tril_ones, la2) + _dot(tril_ones, la3)
        g_last = g_cum[GLA_CHUNK - 1:GLA_CHUNK, :]
        kc = k_s[rows, :]
        q_dec = (q_s[rows, :] * jnp.exp(g_cum)).astype(BF16)
        k_inv = (kc * jnp.exp(-g_cum)).astype(BF16)
        k_end = (kc * jnp.exp(g_last - g_cum)).astype(BF16)
        decay = jnp.exp(g_last)
        for h in range(GLA_HEADS):
            ks = slice(h * HEAD_K_PAD, (h + 1) * HEAD_K_PAD)
            vs = slice(h * HEAD_V_PAD, (h + 1) * HEAD_V_PAD)
            qh = q_dec[:, ks]
            vh = v_s[rows, vs]
            att = _dot_general(qh, k_inv[:, ks], _NT)
            att = jnp.where(causal, att, 0.0).astype(BF16)
            st = state_ref[h]
            o = _dot(jnp.concatenate([qh, att], axis=1), jnp.concatenate([st.astype(BF16), vh], axis=0))
            kv = _dot_general(k_end[:, ks], vh, _TN)
            dcol = jnp.broadcast_to(decay[:, ks], (HEAD_K_PAD, HEAD_K_PAD)).T
            state_ref[h] = st * jnp.concatenate([dcol, dcol], axis=1) + kv
            ms = jnp.sum(o * o, axis=-1, keepdims=True) * (1.0 / GLA_HEAD_V)
            o = o * lax.rsqrt(ms + LN_EPS) * hg_ref[:, vs]
            o_s[rows, vs] = (o * r_s[rows, vs]).astype(BF16)

    mem_q = _dot(xb, wmq_ref[...])
    xa = _memory_xattn(mem_q, kbd_ref, vbd_ref)
    y_mix = _dot(o_s[...], wo_mix_ref[...])
    y_xa = _dot(xa.astype(BF16), wo_xa_ref[...])
    out_ref[...] = _layer_norm_rows(DEEPNORM_ALPHA * x + (y_mix + y_xa), lng_ref[...], lnb_ref[...])


def _conv_layer_kernel(x_ref, mem_ref, wb_ref, wc_ref, wh_ref, wmq_ref, cw_ref, wkt_ref, wvm_ref,
                       wo_mix_ref, wo_xa_ref, lng_ref, lnb_ref,
                       out_ref,
                       kbd_ref, vbd_ref, ch_s):
    tile = x_ref.shape[0]

    @pl.when(pl.program_id(1) == 0)
    def _():
        _build_memory_kv(mem_ref, wkt_ref, wvm_ref, kbd_ref, vbd_ref)
        ch_s[0:HALO, :] = jnp.zeros((HALO, MIX_WIDTH), F32)

    x = x_ref[...]
    xb = x.astype(BF16)
    c_gate = _dot(xb, wc_ref[...])
    h_in = _dot(xb, wh_ref[...])
    ch_s[HALO:HALO + tile, :] = c_gate * h_in
    conv = (ch_s[pl.ds(HALO - 2, tile), :] * cw_ref[0:1, :]
            + ch_s[pl.ds(HALO - 1, tile), :] * cw_ref[1:2, :]
            + ch_s[pl.ds(HALO, tile), :] * cw_ref[2:3, :])
    b_gate = _dot(xb, wb_ref[...])
    mix = (b_gate * conv).astype(BF16)
    ch_s[0:HALO, :] = ch_s[pl.ds(tile, HALO), :]

    mem_q = _dot(xb, wmq_ref[...])
    xa = _memory_xattn(mem_q, kbd_ref, vbd_ref)
    y_mix = _dot(mix, wo_mix_ref[...])
    y_xa = _dot(xa.astype(BF16), wo_xa_ref[...])
    out_ref[...] = _layer_norm_rows(DEEPNORM_ALPHA * x + (y_mix + y_xa), lng_ref[...], lnb_ref[...])


def _ffn_kernel(x_ref, wup_ref, cw_ref, cb_ref, wdn_ref, lng_ref, lnb_ref,
                out_ref,
                tail_ref, u_s, h_s):
    tile = x_ref.shape[0]

    @pl.when(pl.program_id(1) == 0)
    def _():
        tail_ref[...] = jnp.zeros_like(tail_ref)

    x = x_ref[...]
    xb = x.astype(BF16)

    def conv_cols(half, j):
        cols = slice(half * D_FF + j * FFN_CHUNK, half * D_FF + (j + 1) * FFN_CHUNK)
        buf = 2 * (j % 2) + half
        u_s[buf, 0:HALO, :] = tail_ref[:, cols]
        u_s[buf, HALO:HALO + tile, :] = _dot(xb, wup_ref[:, cols])
        tail_ref[:, cols] = u_s[buf, pl.ds(tile, HALO), :]
        return (u_s[buf, pl.ds(HALO - 2, tile), :] * cw_ref[0:1, cols]
                + u_s[buf, pl.ds(HALO - 1, tile), :] * cw_ref[1:2, cols]
                + u_s[buf, pl.ds(HALO, tile), :] * cw_ref[2:3, cols] + cb_ref[:, cols])

    for j in range(D_FF // FFN_CHUNK):
        gate = conv_cols(0, j)
        val = conv_cols(1, j)
        h_s[:, j * FFN_CHUNK:(j + 1) * FFN_CHUNK] = (gate * jax.nn.sigmoid(gate) * val).astype(BF16)

    y = _dot(h_s[...], wdn_ref[...])
    out_ref[...] = _layer_norm_rows(DEEPNORM_ALPHA * x + y, lng_ref[...], lnb_ref[...])


def _full_spec(arr):
    zeros = (0,) * arr.ndim
    return pl.BlockSpec(arr.shape, lambda b, s: zeros, pipeline_mode=pl.Buffered(1))


def _row_spec(tile, width):
    return pl.BlockSpec((None, tile, width), lambda b, s: (b, s, 0))


def _mem_spec():
    return pl.BlockSpec((None, N_MEM, D_MODEL), lambda b, s: (b, 0, 0))


def _compiler_params():
    return pltpu.CompilerParams(dimension_semantics=("arbitrary", "arbitrary"),
                                vmem_limit_bytes=VMEM_LIMIT_BYTES)


def _xattn_scratch():
    return [pltpu.VMEM((XA_WIDTH, XA_HEADS * N_MEM), BF16),
            pltpu.VMEM((XA_HEADS * N_MEM, 2 * XA_WIDTH), BF16)]


def _gla_layer(x, mem, weights):
    bsz, seq, _ = x.shape
    tile = min(MIX_TILE, seq)
    return pl.pallas_call(
        _gla_layer_kernel,
        grid=(bsz, seq // tile),
        in_specs=[_row_spec(tile, D_MODEL), _mem_spec()] + [_full_spec(w) for w in weights],
        out_specs=_row_spec(tile, D_MODEL),
        out_shape=jax.ShapeDtypeStruct(x.shape, F32),
        scratch_shapes=_xattn_scratch() + [
            pltpu.VMEM((GLA_HEADS, HEAD_K_PAD, HEAD_V_PAD), F32),
            pltpu.VMEM((tile, GLA_DK_PAD), F32),
            pltpu.VMEM((tile, GLA_DK_PAD), F32),
            pltpu.VMEM((tile, GLA_DV_PAD), BF16),
            pltpu.VMEM((tile, GLA_DV_PAD), F32),
            pltpu.VMEM((tile, GLA_DK_PAD), F32),
            pltpu.VMEM((tile, GLA_DV_PAD), BF16),
        ],
        compiler_params=_compiler_params(),
        name="gla_layer",
    )(x, mem, *weights)


def _conv_layer(x, mem, weights):
    bsz, seq, _ = x.shape
    tile = min(MIX_TILE, seq)
    return pl.pallas_call(
        _conv_layer_kernel,
        grid=(bsz, seq // tile),
        in_specs=[_row_spec(tile, D_MODEL), _mem_spec()] + [_full_spec(w) for w in weights],
        out_specs=_row_spec(tile, D_MODEL),
        out_shape=jax.ShapeDtypeStruct(x.shape, F32),
        scratch_shapes=_xattn_scratch() + [pltpu.VMEM((HALO + tile, MIX_WIDTH), F32)],
        compiler_params=_compiler_params(),
        name="conv_layer",
    )(x, mem, *weights)


def _ffn_layer(x, weights):
    bsz, seq, _ = x.shape
    tile = min(FFN_TILE, seq)
    return pl.pallas_call(
        _ffn_kernel,
        grid=(bsz, seq // tile),
        in_specs=[_row_spec(tile, D_MODEL)] + [_full_spec(w) for w in weights],
        out_specs=_row_spec(tile, D_MODEL),
        out_shape=jax.ShapeDtypeStruct(x.shape, F32),
        scratch_shapes=[
            pltpu.VMEM((HALO, 2 * D_FF), F32),
            pltpu.VMEM((4, HALO + tile, FFN_CHUNK), F32),
            pltpu.VMEM((tile, D_FF), BF16),
        ],
        compiler_params=_compiler_params(),
        name="conv_ffn",
    )(x, *weights)


def _pad_heads(w, head_width, padded_width):
    lead = w.shape[:-1]
    w = w.reshape(lead + (GLA_HEADS, head_width))
    w = jnp.pad(w, [(0, 0)] * len(lead) + [(0, 0), (0, padded_width - head_width)])
    return w.reshape(lead + (GLA_HEADS * padded_width,))


def _row(v):
    return v.reshape(1, -1)


def _memory_weights(w_mem_kv_i):
    return w_mem_kv_i[:, :XA_WIDTH].T.astype(BF16), w_mem_kv_i[:, XA_WIDTH:].astype(BF16)


def kernel(x, mem, gla_w_in, gla_w_a2, gla_b_a, gla_head_g, gla_w_out, conv_w_in, conv_w, conv_w_out,
           w_mem_kv, ln1_g, ln1_b, ffn_w_up, ffn_conv_w, ffn_conv_b, ffn_w_down, ln2_g, ln2_b):
    for i in range(DEPTH):
        j = i // 2
        wkt, wvm = _memory_weights(w_mem_kv[i])
        if i % 2 == 0:
            w_in = gla_w_in[j]
            o_k, o_v, o_r = GLA_DK, 2 * GLA_DK, 2 * GLA_DK + GLA_DV
            o_g = o_r + GLA_DV
            o_mq = o_g + GLA_GATE_RANK
            wq = _pad_heads(w_in[:, :o_k], GLA_HEAD_K, HEAD_K_PAD)
            wq = wq.at[:, GATE_COL:GATE_COL + GLA_GATE_RANK].set(w_in[:, o_g:o_mq]).astype(BF16)
            wk = _pad_heads(w_in[:, o_k:o_v], GLA_HEAD_K, HEAD_K_PAD).astype(BF16)
            wv = _pad_heads(w_in[:, o_v:o_r], GLA_HEAD_V, HEAD_V_PAD).astype(BF16)
            wr = _pad_heads(w_in[:, o_r:o_g], GLA_HEAD_V, HEAD_V_PAD).astype(BF16)
            wmq = w_in[:, o_mq:].astype(BF16)
            wa2 = jnp.zeros((HEAD_K_PAD, GLA_DK_PAD), F32)
            wa2 = wa2.at[GATE_COL:GATE_COL + GLA_GATE_RANK].set(
                _pad_heads(gla_w_a2[j], GLA_HEAD_K, HEAD_K_PAD)).astype(BF16)
            ba = _row(_pad_heads(gla_b_a[j], GLA_HEAD_K, HEAD_K_PAD))
            hg = _row(_pad_heads(gla_head_g[j], GLA_HEAD_V, HEAD_V_PAD))
            w_out = gla_w_out[j]
            wo_mix = _pad_heads(w_out[:MIX_WIDTH].T, GLA_HEAD_V, HEAD_V_PAD).T.astype(BF16)
            wo_xa = w_out[MIX_WIDTH:].astype(BF16)
            x = _gla_layer(x, mem, (wq, wk, wv, wr, wmq, wa2, ba, hg, wkt, wvm, wo_mix, wo_xa,
                                    _row(ln1_g[i]), _row(ln1_b[i])))
        else:
            w_in = conv_w_in[j]
            wb = w_in[:, :MIX_WIDTH].astype(BF16)
            wc = w_in[:, MIX_WIDTH:2 * MIX_WIDTH].astype(BF16)
            wh = w_in[:, 2 * MIX_WIDTH:3 * MIX_WIDTH].astype(BF16)
            wmq = w_in[:, 3 * MIX_WIDTH:].astype(BF16)
            w_out = conv_w_out[j]
            x = _conv_layer(x, mem, (wb, wc, wh, wmq, conv_w[j], wkt, wvm,
                                     w_out[:MIX_WIDTH].astype(BF16), w_out[MIX_WIDTH:].astype(BF16),
                                     _row(ln1_g[i]), _row(ln1_b[i])))
        x = _ffn_layer(x, (ffn_w_up[i].astype(BF16), ffn_conv_w[i], _row(ffn_conv_b[i]),
                           ffn_w_down[i].astype(BF16), _row(ln2_g[i]), _row(ln2_b[i])))
    return x
```

```python
import functools
import math

import jax
import jax.numpy as jnp
from jax import lax
from jax.experimental import pallas as pl
from jax.experimental.pallas import tpu as pltpu

D_MODEL = 1024
DEPTH = 4
N_MEM = 256
XA_HEADS = 4
XA_WIDTH = D_MODEL // 4
XA_HEAD_DIM = XA_WIDTH // XA_HEADS
MIX_WIDTH = D_MODEL - XA_WIDTH
GLA_HEADS = 4
GLA_DV = MIX_WIDTH
GLA_DK = MIX_WIDTH // 2
GLA_HEAD_K = GLA_DK // GLA_HEADS
GLA_HEAD_V = GLA_DV // GLA_HEADS
GLA_GATE_RANK = 16
GLA_GATE_TAU = 16.0
GLA_CHUNK = 64
CONV_WIDTH = 3
D_FF = int(math.ceil(8 * D_MODEL / 3 / 128)) * 128
LN_EPS = 1e-5
DEEPNORM_ALPHA = (2 * DEPTH) ** 0.25

HEAD_K_PAD = 128
HEAD_V_PAD = 256
GLA_DK_PAD = GLA_HEADS * HEAD_K_PAD
GLA_DV_PAD = GLA_HEADS * HEAD_V_PAD
GATE_COL = GLA_HEAD_K
QA_WIDTH = 2 * HEAD_K_PAD

SUBLANES = 8
GROUP = SUBLANES * SUBLANES

MIX_TILE = 512
FFN_TILE = 512
FFN_CHUNK = 256
VMEM_LIMIT_BYTES = 56 * 1024 * 1024

F32 = jnp.float32
BF16 = jnp.bfloat16
_NT = (((1,), (1,)), ((), ()))
_TN = (((0,), (0,)), ((), ()))

_dot = functools.partial(jnp.dot, preferred_element_type=F32)
_dot_general = functools.partial(lax.dot_general, preferred_element_type=F32)


def _layer_norm_rows(z, g, b):
    mu = jnp.mean(z, axis=-1, keepdims=True)
    zc = z - mu
    var = jnp.mean(zc * zc, axis=-1, keepdims=True)
    return zc * lax.rsqrt(var + LN_EPS) * g + b


def _build_memory_kv(mem_ref, wkt_ref, wv_ref, kbd_ref, vbd_ref):
    memb = mem_ref[...].astype(BF16)
    k_t = _dot_general(wkt_ref[...], memb, _NT)
    v = _dot(memb, wv_ref[...])
    row_head = lax.broadcasted_iota(jnp.int32, (XA_WIDTH, N_MEM), 0) // XA_HEAD_DIM
    col_head = lax.broadcasted_iota(jnp.int32, (N_MEM, XA_WIDTH), 1) // XA_HEAD_DIM
    for h in range(XA_HEADS):
        kbd_ref[:, h * N_MEM:(h + 1) * N_MEM] = jnp.where(row_head == h, k_t, 0.0).astype(BF16)
        vbd_ref[h * N_MEM:(h + 1) * N_MEM, 0:XA_WIDTH] = jnp.where(col_head == h, v, 0.0).astype(BF16)
        vbd_ref[h * N_MEM:(h + 1) * N_MEM, XA_WIDTH:] = jnp.where(col_head == h, 1.0, 0.0).astype(BF16)


def _memory_xattn(mem_q, kbd_ref, vbd_ref):
    q = (mem_q * (XA_HEAD_DIM ** -0.5)).astype(BF16)
    s = _dot(q, kbd_ref[...])
    es = []
    for h in range(XA_HEADS):
        sh = s[:, h * N_MEM:(h + 1) * N_MEM]
        es.append(jnp.exp(sh - jnp.max(sh, axis=-1, keepdims=True)).astype(BF16))
    acc = _dot(jnp.concatenate(es, axis=-1), vbd_ref[...])
    return acc[:, :XA_WIDTH] / acc[:, XA_WIDTH:]


def _stored_position(shape, axis):
    r = lax.broadcasted_iota(jnp.int32, shape, axis)
    return (r % SUBLANES) * SUBLANES + r // SUBLANES


def _causal_taps(load, tile, prev6, prev7):
    first_sublane = lax.broadcasted_iota(jnp.int32, prev7.shape, 0) == 0
    roll6, roll7 = pltpu.roll(prev6, 1, axis=0), pltpu.roll(prev7, 1, axis=0)
    one_back, two_back = [], []
    for g in range(tile // GROUP):
        base = g * GROUP
        slab6 = load(base + GROUP - 2 * SUBLANES, SUBLANES)
        slab7 = load(base + GROUP - SUBLANES, SUBLANES)
        new6, new7 = pltpu.roll(slab6, 1, axis=0), pltpu.roll(slab7, 1, axis=0)
        back7 = jnp.where(first_sublane, roll7, new7)
        back6 = jnp.where(first_sublane, roll6, new6)
        one_back += [back7, load(base, GROUP - SUBLANES)]
        two_back += [back6, back7, load(base, GROUP - 2 * SUBLANES)]
        roll6, roll7 = new6, new7
    return jnp.concatenate(one_back, axis=0), jnp.concatenate(two_back, axis=0), slab6, slab7


def _gla_layer_kernel(x_ref, mem_ref, wq_ref, wk_ref, wv_ref, wr_ref, wmq_ref, wa2_ref,
                      ba_ref, hg_ref, wkt_ref, wvm_ref, wo_mix_ref, wo_xa_ref, lng_ref, lnb_ref,
                      out_ref,
                      kbd_ref, vbd_ref, state_ref, q_s, k_s, r_s, la_s, qa_s, rhs_s, kv_s, dc_s, o_s):
    tile = x_ref.shape[0]
    n_chunks = tile // GLA_CHUNK

    @pl.when(pl.program_id(1) == 0)
    def _():
        _build_memory_kv(mem_ref, wkt_ref, wvm_ref, kbd_ref, vbd_ref)
        state_ref[...] = jnp.zeros_like(state_ref)
        for h in range(GLA_HEADS):
            qa_s[:, h * QA_WIDTH + HEAD_K_PAD + GLA_CHUNK:(h + 1) * QA_WIDTH] = jnp.zeros(
                (tile, QA_WIDTH - HEAD_K_PAD - GLA_CHUNK), BF16)
        rhs_s[:, HEAD_K_PAD + GLA_CHUNK:, :] = jnp.zeros(
            (n_chunks * GLA_HEADS, QA_WIDTH - HEAD_K_PAD - GLA_CHUNK, HEAD_V_PAD), BF16)

    x = x_ref[...]
    xb = x.astype(BF16)
    q_raw = _dot(xb, wq_ref[...])
    q_s[...] = q_raw * (GLA_HEAD_K ** -0.5)
    z = _dot(q_raw[:, 0:HEAD_K_PAD].astype(BF16), wa2_ref[...]) + ba_ref[...]
    la_s[...] = jax.nn.log_sigmoid(z) * (1.0 / GLA_GATE_TAU)
    k_s[...] = _dot(xb, wk_ref[...])
    v = _dot(xb, wv_ref[...]).astype(BF16)
    for c in range(n_chunks):
        for h in range(GLA_HEADS):
            rhs_s[c * GLA_HEADS + h, HEAD_K_PAD:HEAD_K_PAD + GLA_CHUNK, :] = (
                v[c * GLA_CHUNK:(c + 1) * GLA_CHUNK, h * HEAD_V_PAD:(h + 1) * HEAD_V_PAD])
    r = _dot(xb, wr_ref[...])
    r_s[...] = r * jax.nn.sigmoid(r)

    causal = (_stored_position((GLA_CHUNK, GLA_CHUNK), 0) >= _stored_position((GLA_CHUNK, GLA_CHUNK), 1))
    tril_ones = causal.astype(BF16)

    for c in range(n_chunks):
        rows = slice(c * GLA_CHUNK, (c + 1) * GLA_CHUNK)
        la = la_s[rows, :]
        la1 = la.astype(BF16)
        rem = la - la1.astype(F32)
        la2 = rem.astype(BF16)
        la3 = (rem - la2.astype(F32)).astype(BF16)
        g_cum = _dot(tril_ones, la1) + _dot(tril_ones, la2) + _dot(tril_ones, la3)
        g_last = g_cum[GLA_CHUNK - 1:GLA_CHUNK, :]
        kc = k_s[rows, :]
        q_dec = (q_s[rows, :] * jnp.exp(g_cum)).astype(BF16)
        k_inv = (kc * jnp.exp(-g_cum)).astype(BF16)
        k_end = (kc * jnp.exp(g_last - g_cum)).astype(BF16)
        decay = jnp.exp(g_last)
        for h in range(GLA_HEADS):
            ks = slice(h * HEAD_K_PAD, (h + 1) * HEAD_K_PAD)
            ch = c * GLA_HEADS + h
            qh = q_dec[:, ks]
            att = _dot_general(qh, k_inv[:, ks], _NT)
            qa_s[rows, h * QA_WIDTH:h * QA_WIDTH + HEAD_K_PAD] = qh
            qa_s[rows, h * QA_WIDTH + HEAD_K_PAD:h * QA_WIDTH + HEAD_K_PAD + GLA_CHUNK] = (
                jnp.where(causal, att, 0.0).astype(BF16))
            vh = rhs_s[ch, HEAD_K_PAD:HEAD_K_PAD + GLA_CHUNK, :]
            kv_s[ch] = _dot_general(k_end[:, ks], vh, _TN)
            dc_s[ch] = jnp.broadcast_to(decay[:, ks], (HEAD_K_PAD, HEAD_K_PAD)).T

    for c in range(n_chunks):
        for h in range(GLA_HEADS):
            ch = c * GLA_HEADS + h
            st = state_ref[h]
            rhs_s[ch, 0:HEAD_K_PAD, :] = st.astype(BF16)
            dcol = dc_s[ch]
            state_ref[h] = st * jnp.concatenate([dcol, dcol], axis=1) + kv_s[ch]

    for c in range(n_chunks):
        rows = slice(c * GLA_CHUNK, (c + 1) * GLA_CHUNK)
        for h in range(GLA_HEADS):
            vs = slice(h * HEAD_V_PAD, (h + 1) * HEAD_V_PAD)
            o = _dot(qa_s[rows, h * QA_WIDTH:(h + 1) * QA_WIDTH], rhs_s[c * GLA_HEADS + h])
            ms = jnp.sum(o * o, axis=-1, keepdims=True) * (1.0 / GLA_HEAD_V)
            o = o * lax.rsqrt(ms + LN_EPS) * hg_ref[:, vs]
            o_s[rows, vs] = (o * r_s[rows, vs]).astype(BF16)

    mem_q = _dot(xb, wmq_ref[...])
    xa = _memory_xattn(mem_q, kbd_ref, vbd_ref)
    y_mix = _dot(o_s[...], wo_mix_ref[...])
    y_xa = _dot(xa.astype(BF16), wo_xa_ref[...])
    out_ref[...] = _layer_norm_rows(DEEPNORM_ALPHA * x + (y_mix + y_xa), lng_ref[...], lnb_ref[...])


def _conv_layer_kernel(x_ref, mem_ref, wb_ref, wc_ref, wh_ref, wmq_ref, cw_ref, wkt_ref, wvm_ref,
                       wo_mix_ref, wo_xa_ref, lng_ref, lnb_ref,
                       out_ref,
                       kbd_ref, vbd_ref, ch_s, tail_ref):
    tile = x_ref.shape[0]

    @pl.when(pl.program_id(1) == 0)
    def _():
        _build_memory_kv(mem_ref, wkt_ref, wvm_ref, kbd_ref, vbd_ref)
        tail_ref[...] = jnp.zeros_like(tail_ref)

    x = x_ref[...]
    xb = x.astype(BF16)
    c_gate = _dot(xb, wc_ref[...])
    h_in = _dot(xb, wh_ref[...])
    ch_s[...] = c_gate * h_in
    one_back, two_back, last6, last7 = _causal_taps(
        lambda r0, n: ch_s[r0:r0 + n, :], tile, tail_ref[0:SUBLANES, :], tail_ref[SUBLANES:, :])
    tail_ref[0:SUBLANES, :] = last6
    tail_ref[SUBLANES:, :] = last7
    conv = two_back * cw_ref[0:1, :] + one_back * cw_ref[1:2, :] + ch_s[...] * cw_ref[2:3, :]
    b_gate = _dot(xb, wb_ref[...])
    mix = (b_gate * conv).astype(BF16)

    mem_q = _dot(xb, wmq_ref[...])
    xa = _memory_xattn(mem_q, kbd_ref, vbd_ref)
    y_mix = _dot(mix, wo_mix_ref[...])
    y_xa = _dot(xa.astype(BF16), wo_xa_ref[...])
    out_ref[...] = _layer_norm_rows(DEEPNORM_ALPHA * x + (y_mix + y_xa), lng_ref[...], lnb_ref[...])


def _ffn_kernel(x_ref, wup_ref, cw_ref, cb_ref, wdn_ref, lng_ref, lnb_ref,
                out_ref,
                tail_ref, u_s, h_s):
    tile = x_ref.shape[0]

    @pl.when(pl.program_id(1) == 0)
    def _():
        tail_ref[...] = jnp.zeros_like(tail_ref)

    x = x_ref[...]
    xb = x.astype(BF16)

    def conv_cols(half, j):
        cols = slice(half * D_FF + j * FFN_CHUNK, half * D_FF + (j + 1) * FFN_CHUNK)
        buf = 2 * (j % 2) + half
        u_s[buf] = _dot(xb, wup_ref[:, cols])
        one_back, two_back, last6, last7 = _causal_taps(
            lambda r0, n: u_s[buf, r0:r0 + n, :], tile, tail_ref[0:SUBLANES, cols], tail_ref[SUBLANES:, cols])
        tail_ref[0:SUBLANES, cols] = last6
        tail_ref[SUBLANES:, cols] = last7
        return (two_back * cw_ref[0:1, cols] + one_back * cw_ref[1:2, cols]
                + u_s[buf] * cw_ref[2:3, cols] + cb_ref[:, cols])

    for j in range(D_FF // FFN_CHUNK):
        gate = conv_cols(0, j)
        val = conv_cols(1, j)
        h_s[:, j * FFN_CHUNK:(j + 1) * FFN_CHUNK] = (gate * jax.nn.sigmoid(gate) * val).astype(BF16)

    y = _dot(h_s[...], wdn_ref[...])
    out_ref[...] = _layer_norm_rows(DEEPNORM_ALPHA * x + y, lng_ref[...], lnb_ref[...])


def _full_spec(arr):
    zeros = (0,) * arr.ndim
    return pl.BlockSpec(arr.shape, lambda b, s: zeros, pipeline_mode=pl.Buffered(1))


def _row_spec(tile, width):
    return pl.BlockSpec((None, tile, width), lambda b, s: (b, s, 0))


def _mem_spec():
    return pl.BlockSpec((None, N_MEM, D_MODEL), lambda b, s: (b, 0, 0))


def _compiler_params():
    return pltpu.CompilerParams(dimension_semantics=("arbitrary", "arbitrary"),
                                vmem_limit_bytes=VMEM_LIMIT_BYTES)


def _xattn_scratch():
    return [pltpu.VMEM((XA_WIDTH, XA_HEADS * N_MEM), BF16),
            pltpu.VMEM((XA_HEADS * N_MEM, 2 * XA_WIDTH), BF16)]


def _gla_layer(x, mem, weights):
    bsz, seq, _ = x.shape
    tile = min(MIX_TILE, seq)
    n_ch = tile // GLA_CHUNK
    return pl.pallas_call(
        _gla_layer_kernel,
        grid=(bsz, seq // tile),
        in_specs=[_row_spec(tile, D_MODEL), _mem_spec()] + [_full_spec(w) for w in weights],
        out_specs=_row_spec(tile, D_MODEL),
        out_shape=jax.ShapeDtypeStruct(x.shape, F32),
        scratch_shapes=_xattn_scratch() + [
            pltpu.VMEM((GLA_HEADS, HEAD_K_PAD, HEAD_V_PAD), F32),
            pltpu.VMEM((tile, GLA_DK_PAD), F32),
            pltpu.VMEM((tile, GLA_DK_PAD), F32),
            pltpu.VMEM((tile, GLA_DV_PAD), F32),
            pltpu.VMEM((tile, GLA_DK_PAD), F32),
            pltpu.VMEM((tile, GLA_HEADS * QA_WIDTH), BF16),
            pltpu.VMEM((n_ch * GLA_HEADS, QA_WIDTH, HEAD_V_PAD), BF16),
            pltpu.VMEM((n_ch * GLA_HEADS, HEAD_K_PAD, HEAD_V_PAD), F32),
            pltpu.VMEM((n_ch * GLA_HEADS, HEAD_K_PAD, HEAD_K_PAD), F32),
            pltpu.VMEM((tile, GLA_DV_PAD), BF16),
        ],
        compiler_params=_compiler_params(),
        name="gla_layer",
    )(x, mem, *weights)


def _conv_layer(x, mem, weights):
    bsz, seq, _ = x.shape
    tile = min(MIX_TILE, seq)
    return pl.pallas_call(
        _conv_layer_kernel,
        grid=(bsz, seq // tile),
        in_specs=[_row_spec(tile, D_MODEL), _mem_spec()] + [_full_spec(w) for w in weights],
        out_specs=_row_spec(tile, D_MODEL),
        out_shape=jax.ShapeDtypeStruct(x.shape, F32),
        scratch_shapes=_xattn_scratch() + [
            pltpu.VMEM((tile, MIX_WIDTH), F32),
            pltpu.VMEM((2 * SUBLANES, MIX_WIDTH), F32),
        ],
        compiler_params=_compiler_params(),
        name="conv_layer",
    )(x, mem, *weights)


def _ffn_layer(x, weights):
    bsz, seq, _ = x.shape
    tile = min(FFN_TILE, seq)
    return pl.pallas_call(
        _ffn_kernel,
        grid=(bsz, seq // tile),
        in_specs=[_row_spec(tile, D_MODEL)] + [_full_spec(w) for w in weights],
        out_specs=_row_spec(tile, D_MODEL),
        out_shape=jax.ShapeDtypeStruct(x.shape, F32),
        scratch_shapes=[
            pltpu.VMEM((2 * SUBLANES, 2 * D_FF), F32),
            pltpu.VMEM((4, tile, FFN_CHUNK), F32),
            pltpu.VMEM((tile, D_FF), BF16),
        ],
        compiler_params=_compiler_params(),
        name="conv_ffn",
    )(x, *weights)


def _pad_heads(w, head_width, padded_width):
    lead = w.shape[:-1]
    w = w.reshape(lead + (GLA_HEADS, head_width))
    w = jnp.pad(w, [(0, 0)] * len(lead) + [(0, 0), (0, padded_width - head_width)])
    return w.reshape(lead + (GLA_HEADS * padded_width,))


def _row(v):
    return v.reshape(1, -1)


def _to_stored_order(t):
    bsz, seq, d = t.shape
    return t.reshape(bsz, seq // GROUP, SUBLANES, SUBLANES, d).swapaxes(2, 3).reshape(bsz, seq, d)


def _memory_weights(w_mem_kv_i):
    return w_mem_kv_i[:, :XA_WIDTH].T.astype(BF16), w_mem_kv_i[:, XA_WIDTH:].astype(BF16)


def kernel(x, mem, gla_w_in, gla_w_a2, gla_b_a, gla_head_g, gla_w_out, conv_w_in, conv_w, conv_w_out,
           w_mem_kv, ln1_g, ln1_b, ffn_w_up, ffn_conv_w, ffn_conv_b, ffn_w_down, ln2_g, ln2_b):
    x = _to_stored_order(x)
    for i in range(DEPTH):
        j = i // 2
        wkt, wvm = _memory_weights(w_mem_kv[i])
        if i % 2 == 0:
            w_in = gla_w_in[j]
            o_k, o_v, o_r = GLA_DK, 2 * GLA_DK, 2 * GLA_DK + GLA_DV
            o_g = o_r + GLA_DV
            o_mq = o_g + GLA_GATE_RANK
            wq = _pad_heads(w_in[:, :o_k], GLA_HEAD_K, HEAD_K_PAD)
            wq = wq.at[:, GATE_COL:GATE_COL + GLA_GATE_RANK].set(w_in[:, o_g:o_mq]).astype(BF16)
            wk = _pad_heads(w_in[:, o_k:o_v], GLA_HEAD_K, HEAD_K_PAD).astype(BF16)
            wv = _pad_heads(w_in[:, o_v:o_r], GLA_HEAD_V, HEAD_V_PAD).astype(BF16)
            wr = _pad_heads(w_in[:, o_r:o_g], GLA_HEAD_V, HEAD_V_PAD).astype(BF16)
            wmq = w_in[:, o_mq:].astype(BF16)
            wa2 = jnp.zeros((HEAD_K_PAD, GLA_DK_PAD), F32)
            wa2 = wa2.at[GATE_COL:GATE_COL + GLA_GATE_RANK].set(
                _pad_heads(gla_w_a2[j], GLA_HEAD_K, HEAD_K_PAD)).astype(BF16)
            ba = _row(_pad_heads(gla_b_a[j], GLA_HEAD_K, HEAD_K_PAD))
            hg = _row(_pad_heads(gla_head_g[j], GLA_HEAD_V, HEAD_V_PAD))
            w_out = gla_w_out[j]
            wo_mix = _pad_heads(w_out[:MIX_WIDTH].T, GLA_HEAD_V, HEAD_V_PAD).T.astype(BF16)
            wo_xa = w_out[MIX_WIDTH:].astype(BF16)
            x = _gla_layer(x, mem, (wq, wk, wv, wr, wmq, wa2, ba, hg, wkt, wvm, wo_mix, wo_xa,
                                    _row(ln1_g[i]), _row(ln1_b[i])))
        else:
            w_in = conv_w_in[j]
            wb = w_in[:, :MIX_WIDTH].astype(BF16)
            wc = w_in[:, MIX_WIDTH:2 * MIX_WIDTH].astype(BF16)
            wh = w_in[:, 2 * MIX_WIDTH:3 * MIX_WIDTH].astype(BF16)
            wmq = w_in[:, 3 * MIX_WIDTH:].astype(BF16)
            w_out = conv_w_out[j]
            x = _conv_layer(x, mem, (wb, wc, wh, wmq, conv_w[j], wkt, wvm,
                                     w_out[:MIX_WIDTH].astype(BF16), w_out[MIX_WIDTH:].astype(BF16),
                                     _row(ln1_g[i]), _row(ln1_b[i])))
        x = _ffn_layer(x, (ffn_w_up[i].astype(BF16), ffn_conv_w[i], _row(ffn_conv_b[i]),
                           ffn_w_down[i].astype(BF16), _row(ln2_g[i]), _row(ln2_b[i])))
    return _to_stored_order(x)
```

```python
import functools
import math

import jax
import jax.numpy as jnp
from jax import lax
from jax.experimental import pallas as pl
from jax.experimental.pallas import tpu as pltpu

D_MODEL = 1024
DEPTH = 4
N_MEM = 256
XA_HEADS = 4
XA_WIDTH = D_MODEL // 4
XA_HEAD_DIM = XA_WIDTH // XA_HEADS
MIX_WIDTH = D_MODEL - XA_WIDTH
GLA_HEADS = 4
GLA_DV = MIX_WIDTH
GLA_DK = MIX_WIDTH // 2
GLA_HEAD_K = GLA_DK // GLA_HEADS
GLA_HEAD_V = GLA_DV // GLA_HEADS
GLA_GATE_RANK = 16
GLA_GATE_TAU = 16.0
GLA_CHUNK = 64
CONV_WIDTH = 3
D_FF = int(math.ceil(8 * D_MODEL / 3 / 128)) * 128
LN_EPS = 1e-5
DEEPNORM_ALPHA = (2 * DEPTH) ** 0.25

HEAD_K_PAD = 128
HEAD_V_PAD = 256
GLA_DK_PAD = GLA_HEADS * HEAD_K_PAD
GLA_DV_PAD = GLA_HEADS * HEAD_V_PAD
GATE_COL = GLA_HEAD_K
QA_WIDTH = 2 * HEAD_K_PAD
GLA_COL_K = GLA_DK_PAD
GLA_COL_V = GLA_COL_K + GLA_DK_PAD
GLA_COL_R = GLA_COL_V + GLA_DV_PAD
GLA_COL_MQ = GLA_COL_R + GLA_DV_PAD
GLA_IN_PAD = GLA_COL_MQ + XA_WIDTH

SUBLANES = 8
GROUP = SUBLANES * SUBLANES

MIX_TILE = 512
FFN_TILE = 512
FFN_CHUNK = 256
VMEM_LIMIT_BYTES = 56 * 1024 * 1024

F32 = jnp.float32
BF16 = jnp.bfloat16
_NT = (((1,), (1,)), ((), ()))
_TN = (((0,), (0,)), ((), ()))

_dot = functools.partial(jnp.dot, preferred_element_type=F32)
_dot_general = functools.partial(lax.dot_general, preferred_element_type=F32)


def _layer_norm_rows(z, g, b):
    mu = jnp.mean(z, axis=-1, keepdims=True)
    zc = z - mu
    var = jnp.mean(zc * zc, axis=-1, keepdims=True)
    return zc * lax.rsqrt(var + LN_EPS) * g + b


def _deferred_norm_kernel(start_of_row, tile_body, n_tiles, total, n_in, *refs):
    lng_ref, lnb_ref = refs[n_in - 2], refs[n_in - 1]
    out_ref, z_s = refs[n_in], refs[-1]
    i = pl.program_id(0)

    @pl.when(i == 0)
    def _():
        z_s[...] = jnp.zeros_like(z_s)

    @pl.when(i % n_tiles == 0)
    def _():
        start_of_row(*refs)

    @pl.when(i < total)
    def _():
        out_ref[...] = _layer_norm_rows(z_s[...], lng_ref[...], lnb_ref[...])
        tile_body(*refs)

    @pl.when(i == total)
    def _():
        out_ref[...] = _layer_norm_rows(z_s[...], lng_ref[...], lnb_ref[...])


def _build_memory_kv(mem_ref, wkv_ref, kbd_ref, vbd_ref):
    memb = mem_ref[...].astype(BF16)
    kv = _dot(memb, wkv_ref[...])
    k_t = kv[:, :XA_WIDTH].T
    v = kv[:, XA_WIDTH:]
    row_head = lax.broadcasted_iota(jnp.int32, (XA_WIDTH, N_MEM), 0) // XA_HEAD_DIM
    col_head = lax.broadcasted_iota(jnp.int32, (N_MEM, XA_WIDTH), 1) // XA_HEAD_DIM
    for h in range(XA_HEADS):
        kbd_ref[:, h * N_MEM:(h + 1) * N_MEM] = jnp.where(row_head == h, k_t, 0.0).astype(BF16)
        vbd_ref[h * N_MEM:(h + 1) * N_MEM, 0:XA_WIDTH] = jnp.where(col_head == h, v, 0.0).astype(BF16)
        vbd_ref[h * N_MEM:(h + 1) * N_MEM, XA_WIDTH:] = jnp.where(col_head == h, 1.0, 0.0).astype(BF16)


def _memory_xattn(mem_q, kbd_ref, vbd_ref):
    q = (mem_q * (XA_HEAD_DIM ** -0.5)).astype(BF16)
    s = _dot(q, kbd_ref[...])
    es = []
    for h in range(XA_HEADS):
        sh = s[:, h * N_MEM:(h + 1) * N_MEM]
        es.append(jnp.exp(sh - jnp.max(sh, axis=-1, keepdims=True)).astype(BF16))
    acc = _dot(jnp.concatenate(es, axis=-1), vbd_ref[...])
    return acc[:, :XA_WIDTH] / acc[:, XA_WIDTH:]


def _stored_position(shape, axis):
    r = lax.broadcasted_iota(jnp.int32, shape, axis)
    return (r % SUBLANES) * SUBLANES + r // SUBLANES


def _causal_taps(load, tile, prev6, prev7):
    first_sublane = lax.broadcasted_iota(jnp.int32, prev7.shape, 0) == 0
    roll6, roll7 = pltpu.roll(prev6, 1, axis=0), pltpu.roll(prev7, 1, axis=0)
    one_back, two_back = [], []
    for g in range(tile // GROUP):
        base = g * GROUP
        slab6 = load(base + GROUP - 2 * SUBLANES, SUBLANES)
        slab7 = load(base + GROUP - SUBLANES, SUBLANES)
        new6, new7 = pltpu.roll(slab6, 1, axis=0), pltpu.roll(slab7, 1, axis=0)
        back7 = jnp.where(first_sublane, roll7, new7)
        back6 = jnp.where(first_sublane, roll6, new6)
        one_back += [back7, load(base, GROUP - SUBLANES)]
        two_back += [back6, back7, load(base, GROUP - 2 * SUBLANES)]
        roll6, roll7 = new6, new7
    return jnp.concatenate(one_back, axis=0), jnp.concatenate(two_back, axis=0), slab6, slab7


def _gla_start_of_row(x_ref, mem_ref, w_in_ref, wa2_ref, ba_ref, hg_ref, wkv_ref, wo_ref, lng_ref, lnb_ref,
                      out_ref,
                      kbd_ref, vbd_ref, state_ref, q_s, k_s, r_s, la_s, qa_s, rhs_s, kv_s, dc_s, o_s, z_s):
    _build_memory_kv(mem_ref, wkv_ref, kbd_ref, vbd_ref)
    state_ref[...] = jnp.zeros_like(state_ref)
    pad = QA_WIDTH - HEAD_K_PAD - GLA_CHUNK
    for h in range(GLA_HEADS):
        qa_s[:, (h + 1) * QA_WIDTH - pad:(h + 1) * QA_WIDTH] = jnp.zeros((qa_s.shape[0], pad), BF16)
    rhs_s[:, QA_WIDTH - pad:, :] = jnp.zeros((rhs_s.shape[0], pad, HEAD_V_PAD), BF16)


def _gla_tile_body(x_ref, mem_ref, w_in_ref, wa2_ref, ba_ref, hg_ref, wkv_ref, wo_ref, lng_ref, lnb_ref,
                   out_ref,
                   kbd_ref, vbd_ref, state_ref, q_s, k_s, r_s, la_s, qa_s, rhs_s, kv_s, dc_s, o_s, z_s):
    tile = x_ref.shape[0]
    n_chunks = tile // GLA_CHUNK

    x = x_ref[...]
    xb = x.astype(BF16)
    q_raw = _dot(xb, w_in_ref[:, 0:GLA_COL_K])
    q_s[...] = q_raw * (GLA_HEAD_K ** -0.5)
    z = _dot(q_raw[:, 0:HEAD_K_PAD].astype(BF16), wa2_ref[...]) + ba_ref[...]
    la_s[...] = jax.nn.log_sigmoid(z) * (1.0 / GLA_GATE_TAU)
    k_s[...] = _dot(xb, w_in_ref[:, GLA_COL_K:GLA_COL_V])
    v = _dot(xb, w_in_ref[:, GLA_COL_V:GLA_COL_R]).astype(BF16)
    for c in range(n_chunks):
        for h in range(GLA_HEADS):
            rhs_s[c * GLA_HEADS + h, HEAD_K_PAD:HEAD_K_PAD + GLA_CHUNK, :] = (
                v[c * GLA_CHUNK:(c + 1) * GLA_CHUNK, h * HEAD_V_PAD:(h + 1) * HEAD_V_PAD])
    r = _dot(xb, w_in_ref[:, GLA_COL_R:GLA_COL_MQ])
    r_s[...] = r * jax.nn.sigmoid(r)

    causal = (_stored_position((GLA_CHUNK, GLA_CHUNK), 0) >= _stored_position((GLA_CHUNK, GLA_CHUNK), 1))
    tril_ones = causal.astype(BF16)

    for c in range(n_chunks):
        rows = slice(c * GLA_CHUNK, (c + 1) * GLA_CHUNK)
        la = la_s[rows, :]
        la1 = la.astype(BF16)
        rem = la - la1.astype(F32)
        la2 = rem.astype(BF16)
        la3 = (rem - la2.astype(F32)).astype(BF16)
        g_cum = _dot(tril_ones, la1) + _dot(tril_ones, la2) + _dot(tril_ones, la3)
        g_last = g_cum[GLA_CHUNK - 1:GLA_CHUNK, :]
        kc = k_s[rows, :]
        q_dec = (q_s[rows, :] * jnp.exp(g_cum)).astype(BF16)
        k_inv = (kc * jnp.exp(-g_cum)).astype(BF16)
        k_end = (kc * jnp.exp(g_last - g_cum)).astype(BF16)
        decay = jnp.exp(g_last)
        for h in range(GLA_HEADS):
            ks = slice(h * HEAD_K_PAD, (h + 1) * HEAD_K_PAD)
            ch = c * GLA_HEADS + h
            qh = q_dec[:, ks]
            att = _dot_general(qh, k_inv[:, ks], _NT)
            qa_s[rows, h * QA_WIDTH:h * QA_WIDTH + HEAD_K_PAD] = qh
            qa_s[rows, h * QA_WIDTH + HEAD_K_PAD:h * QA_WIDTH + HEAD_K_PAD + GLA_CHUNK] = (
                jnp.where(causal, att, 0.0).astype(BF16))
            vh = rhs_s[ch, HEAD_K_PAD:HEAD_K_PAD + GLA_CHUNK, :]
            kv_s[ch] = _dot_general(k_end[:, ks], vh, _TN)
            dc_s[ch] = jnp.broadcast_to(decay[:, ks], (HEAD_K_PAD, HEAD_K_PAD)).T

    for c in range(n_chunks):
        for h in range(GLA_HEADS):
            ch = c * GLA_HEADS + h
            st = state_ref[h]
            rhs_s[ch, 0:HEAD_K_PAD, :] = st.astype(BF16)
            dcol = dc_s[ch]
            state_ref[h] = st * jnp.concatenate([dcol, dcol], axis=1) + kv_s[ch]

    for c in range(n_chunks):
        rows = slice(c * GLA_CHUNK, (c + 1) * GLA_CHUNK)
        for h in range(GLA_HEADS):
            vs = slice(h * HEAD_V_PAD, (h + 1) * HEAD_V_PAD)
            o = _dot(qa_s[rows, h * QA_WIDTH:(h + 1) * QA_WIDTH], rhs_s[c * GLA_HEADS + h])
            ms = jnp.sum(o * o, axis=-1, keepdims=True) * (1.0 / GLA_HEAD_V)
            o = o * lax.rsqrt(ms + LN_EPS) * hg_ref[:, vs]
            o_s[rows, vs] = (o * r_s[rows, vs]).astype(BF16)

    mem_q = _dot(xb, w_in_ref[:, GLA_COL_MQ:GLA_IN_PAD])
    xa = _memory_xattn(mem_q, kbd_ref, vbd_ref)
    y_mix = _dot(o_s[...], wo_ref[0:GLA_DV_PAD, :])
    y_xa = _dot(xa.astype(BF16), wo_ref[GLA_DV_PAD:, :])
    z_s[...] = DEEPNORM_ALPHA * x + (y_mix + y_xa)


def _conv_start_of_row(x_ref, mem_ref, w_in_ref, cw_ref, wkv_ref, wo_ref, lng_ref, lnb_ref,
                       out_ref,
                       kbd_ref, vbd_ref, ch_s, tail_ref, z_s):
    _build_memory_kv(mem_ref, wkv_ref, kbd_ref, vbd_ref)
    tail_ref[...] = jnp.zeros_like(tail_ref)


def _conv_tile_body(x_ref, mem_ref, w_in_ref, cw_ref, wkv_ref, wo_ref, lng_ref, lnb_ref,
                    out_ref,
                    kbd_ref, vbd_ref, ch_s, tail_ref, z_s):
    tile = x_ref.shape[0]
    x = x_ref[...]
    xb = x.astype(BF16)
    c_gate = _dot(xb, w_in_ref[:, MIX_WIDTH:2 * MIX_WIDTH])
    h_in = _dot(xb, w_in_ref[:, 2 * MIX_WIDTH:3 * MIX_WIDTH])
    ch_s[...] = c_gate * h_in
    one_back, two_back, last6, last7 = _causal_taps(
        lambda r0, n: ch_s[r0:r0 + n, :], tile, tail_ref[0:SUBLANES, :], tail_ref[SUBLANES:, :])
    tail_ref[0:SUBLANES, :] = last6
    tail_ref[SUBLANES:, :] = last7
    conv = two_back * cw_ref[0:1, :] + one_back * cw_ref[1:2, :] + ch_s[...] * cw_ref[2:3, :]
    b_gate = _dot(xb, w_in_ref[:, 0:MIX_WIDTH])
    mix = (b_gate * conv).astype(BF16)

    mem_q = _dot(xb, w_in_ref[:, 3 * MIX_WIDTH:])
    xa = _memory_xattn(mem_q, kbd_ref, vbd_ref)
    y_mix = _dot(mix, wo_ref[0:MIX_WIDTH, :])
    y_xa = _dot(xa.astype(BF16), wo_ref[MIX_WIDTH:, :])
    z_s[...] = DEEPNORM_ALPHA * x + (y_mix + y_xa)


def _ffn_start_of_row(x_ref, wup_ref, cw_ref, cb_ref, wdn_ref, lng_ref, lnb_ref,
                      out_ref,
                      tail_ref, u_s, h_s, z_s):
    tail_ref[...] = jnp.zeros_like(tail_ref)


def _ffn_tile_body(x_ref, wup_ref, cw_ref, cb_ref, wdn_ref, lng_ref, lnb_ref,
                   out_ref,
                   tail_ref, u_s, h_s, z_s):
    tile = x_ref.shape[0]
    x = x_ref[...]
    xb = x.astype(BF16)

    def conv_cols(half, j):
        cols = slice(half * D_FF + j * FFN_CHUNK, half * D_FF + (j + 1) * FFN_CHUNK)
        buf = 2 * (j % 2) + half
        u_s[buf] = _dot(xb, wup_ref[:, cols])
        one_back, two_back, last6, last7 = _causal_taps(
            lambda r0, n: u_s[buf, r0:r0 + n, :], tile, tail_ref[0:SUBLANES, cols], tail_ref[SUBLANES:, cols])
        tail_ref[0:SUBLANES, cols] = last6
        tail_ref[SUBLANES:, cols] = last7
        return (two_back * cw_ref[0:1, cols] + one_back * cw_ref[1:2, cols]
                + u_s[buf] * cw_ref[2:3, cols] + cb_ref[:, cols])

    for j in range(D_FF // FFN_CHUNK):
        gate = conv_cols(0, j)
        val = conv_cols(1, j)
        h_s[:, j * FFN_CHUNK:(j + 1) * FFN_CHUNK] = (gate * jax.nn.sigmoid(gate) * val).astype(BF16)

    z_s[...] = DEEPNORM_ALPHA * x + _dot(h_s[...], wdn_ref[...])


def _layer_spec(arr, layer):
    zeros = (0,) * (arr.ndim - 1)
    return pl.BlockSpec((None,) + arr.shape[1:], lambda i: (layer,) + zeros, pipeline_mode=pl.Buffered(1))


def _layer_call(start_of_row, tile_body, name, x, row_inputs, layer_inputs, tile, scratch_shapes):
    bsz, seq, _ = x.shape
    n_tiles = seq // tile
    total = bsz * n_tiles

    def tile_of(i):
        t = jnp.minimum(i, total - 1)
        return t // n_tiles, t % n_tiles

    def x_map(i):
        b, s = tile_of(i)
        return b, s, 0

    def out_map(i):
        b, s = tile_of(jnp.maximum(i - 1, 0))
        return b, s, 0

    def row_map(i):
        return tile_of(i)[0], 0, 0

    in_specs = ([pl.BlockSpec((None, tile, D_MODEL), x_map)]
                + [pl.BlockSpec((None,) + a.shape[1:], row_map) for a in row_inputs]
                + [_layer_spec(a, layer) for a, layer in layer_inputs])
    n_in = len(in_specs)
    return pl.pallas_call(
        functools.partial(_deferred_norm_kernel, start_of_row, tile_body, n_tiles, total, n_in),
        grid=(total + 1,),
        in_specs=in_specs,
        out_specs=pl.BlockSpec((None, tile, D_MODEL), out_map),
        out_shape=jax.ShapeDtypeStruct(x.shape, F32),
        scratch_shapes=scratch_shapes + [pltpu.VMEM((tile, D_MODEL), F32)],
        compiler_params=pltpu.CompilerParams(dimension_semantics=("arbitrary",),
                                             vmem_limit_bytes=VMEM_LIMIT_BYTES),
        name=name,
    )(x, *row_inputs, *[a for a, _ in layer_inputs])


def _xattn_scratch():
    return [pltpu.VMEM((XA_WIDTH, XA_HEADS * N_MEM), BF16),
            pltpu.VMEM((XA_HEADS * N_MEM, 2 * XA_WIDTH), BF16)]


def _gla_layer(x, mem, layer_inputs):
    tile = min(MIX_TILE, x.shape[1])
    n_ch = tile // GLA_CHUNK
    scratch = _xattn_scratch() + [
        pltpu.VMEM((GLA_HEADS, HEAD_K_PAD, HEAD_V_PAD), F32),
        pltpu.VMEM((tile, GLA_DK_PAD), F32),
        pltpu.VMEM((tile, GLA_DK_PAD), F32),
        pltpu.VMEM((tile, GLA_DV_PAD), F32),
        pltpu.VMEM((tile, GLA_DK_PAD), F32),
        pltpu.VMEM((tile, GLA_HEADS * QA_WIDTH), BF16),
        pltpu.VMEM((n_ch * GLA_HEADS, QA_WIDTH, HEAD_V_PAD), BF16),
        pltpu.VMEM((n_ch * GLA_HEADS, HEAD_K_PAD, HEAD_V_PAD), F32),
        pltpu.VMEM((n_ch * GLA_HEADS, HEAD_K_PAD, HEAD_K_PAD), F32),
        pltpu.VMEM((tile, GLA_DV_PAD), BF16),
    ]
    return _layer_call(_gla_start_of_row, _gla_tile_body, "gla_layer", x, [mem], layer_inputs, tile, scratch)


def _conv_layer(x, mem, layer_inputs):
    tile = min(MIX_TILE, x.shape[1])
    scratch = _xattn_scratch() + [
        pltpu.VMEM((tile, MIX_WIDTH), F32),
        pltpu.VMEM((2 * SUBLANES, MIX_WIDTH), F32),
    ]
    return _layer_call(_conv_start_of_row, _conv_tile_body, "conv_layer", x, [mem], layer_inputs, tile, scratch)


def _ffn_layer(x, layer_inputs):
    tile = min(FFN_TILE, x.shape[1])
    scratch = [
        pltpu.VMEM((2 * SUBLANES, 2 * D_FF), F32),
        pltpu.VMEM((4, tile, FFN_CHUNK), F32),
        pltpu.VMEM((tile, D_FF), BF16),
    ]
    return _layer_call(_ffn_start_of_row, _ffn_tile_body, "conv_ffn", x, [], layer_inputs, tile, scratch)


def _pad_heads(w, head_width, padded_width, axis=-1):
    axis = axis % w.ndim
    shape = w.shape
    w = w.reshape(shape[:axis] + (GLA_HEADS, head_width) + shape[axis + 1:])
    pads = [(0, 0)] * w.ndim
    pads[axis + 1] = (0, padded_width - head_width)
    return jnp.pad(w, pads).reshape(shape[:axis] + (GLA_HEADS * padded_width,) + shape[axis + 1:])


def _rows(p):
    return p.reshape(p.shape[0], 1, p.shape[1])


def _to_stored_order(t):
    bsz, seq, d = t.shape
    return t.reshape(bsz, seq // GROUP, SUBLANES, SUBLANES, d).swapaxes(2, 3).reshape(bsz, seq, d)


def _pack_gla_weights(gla_w_in, gla_w_a2, gla_b_a, gla_head_g, gla_w_out):
    n_layers = gla_w_in.shape[0]
    o_k, o_v, o_r = GLA_DK, 2 * GLA_DK, 2 * GLA_DK + GLA_DV
    o_g = o_r + GLA_DV
    o_mq = o_g + GLA_GATE_RANK
    q = gla_w_in[..., :o_k].reshape(n_layers, D_MODEL, GLA_HEADS, GLA_HEAD_K)
    q_pad = jnp.pad(gla_w_in[..., o_g:o_mq][:, :, None, :],
                    ((0, 0), (0, 0), (0, GLA_HEADS - 1), (0, HEAD_K_PAD - GLA_HEAD_K - GLA_GATE_RANK)))
    wq = jnp.concatenate([q, q_pad], axis=-1).reshape(n_layers, D_MODEL, GLA_DK_PAD)
    w_in = jnp.concatenate([wq,
                            _pad_heads(gla_w_in[..., o_k:o_v], GLA_HEAD_K, HEAD_K_PAD),
                            _pad_heads(gla_w_in[..., o_v:o_r], GLA_HEAD_V, HEAD_V_PAD),
                            _pad_heads(gla_w_in[..., o_r:o_g], GLA_HEAD_V, HEAD_V_PAD),
                            gla_w_in[..., o_mq:]], axis=-1).astype(BF16)
    wa2 = jnp.pad(_pad_heads(gla_w_a2, GLA_HEAD_K, HEAD_K_PAD),
                  ((0, 0), (GATE_COL, HEAD_K_PAD - GATE_COL - GLA_GATE_RANK), (0, 0))).astype(BF16)
    ba = _rows(_pad_heads(gla_b_a, GLA_HEAD_K, HEAD_K_PAD))
    hg = _rows(_pad_heads(gla_head_g, GLA_HEAD_V, HEAD_V_PAD))
    wo = jnp.concatenate([_pad_heads(gla_w_out[:, :MIX_WIDTH, :], GLA_HEAD_V, HEAD_V_PAD, axis=1),
                          gla_w_out[:, MIX_WIDTH:, :]], axis=1).astype(BF16)
    return w_in, wa2, ba, hg, wo


def kernel(x, mem, gla_w_in, gla_w_a2, gla_b_a, gla_head_g, gla_w_out, conv_w_in, conv_w, conv_w_out,
           w_mem_kv, ln1_g, ln1_b, ffn_w_up, ffn_conv_w, ffn_conv_b, ffn_w_down, ln2_g, ln2_b):
    gla_in, gla_a2, gla_ba, gla_hg, gla_wo = _pack_gla_weights(gla_w_in, gla_w_a2, gla_b_a, gla_head_g, gla_w_out)
    conv_in, conv_wo = conv_w_in.astype(BF16), conv_w_out.astype(BF16)
    wkv = w_mem_kv.astype(BF16)
    w_up, w_down = ffn_w_up.astype(BF16), ffn_w_down.astype(BF16)
    ln1g, ln1b, ln2g, ln2b, ffn_cb = _rows(ln1_g), _rows(ln1_b), _rows(ln2_g), _rows(ln2_b), _rows(ffn_conv_b)

    x = _to_stored_order(x)
    for i in range(DEPTH):
        j = i // 2
        if i % 2 == 0:
            x = _gla_layer(x, mem, [(gla_in, j), (gla_a2, j), (gla_ba, j), (gla_hg, j), (wkv, i), (gla_wo, j),
                                    (ln1g, i), (ln1b, i)])
        else:
            x = _conv_layer(x, mem, [(conv_in, j), (conv_w, j), (wkv, i), (conv_wo, j), (ln1g, i), (ln1b, i)])
        x = _ffn_layer(x, [(w_up, i), (ffn_conv_w, i), (ffn_cb, i), (w_down, i), (ln2g, i), (ln2b, i)])
    return _to_stored_order(x)
```

```python
import functools
import math

import jax
import jax.numpy as jnp
from jax import lax
from jax.experimental import pallas as pl
from jax.experimental.pallas import tpu as pltpu

D_MODEL = 1024
DEPTH = 4
N_MEM = 256
XA_HEADS = 4
XA_WIDTH = D_MODEL // 4
XA_HEAD_DIM = XA_WIDTH // XA_HEADS
MIX_WIDTH = D_MODEL - XA_WIDTH
GLA_HEADS = 4
GLA_DV = MIX_WIDTH
GLA_DK = MIX_WIDTH // 2
GLA_HEAD_K = GLA_DK // GLA_HEADS
GLA_HEAD_V = GLA_DV // GLA_HEADS
GLA_GATE_RANK = 16
GLA_GATE_TAU = 16.0
GLA_CHUNK = 64
CONV_WIDTH = 3
D_FF = int(math.ceil(8 * D_MODEL / 3 / 128)) * 128
LN_EPS = 1e-5
DEEPNORM_ALPHA = (2 * DEPTH) ** 0.25

HEAD_K_PAD = 128
HEAD_V_PAD = 256
GLA_DK_PAD = GLA_HEADS * HEAD_K_PAD
GLA_DV_PAD = GLA_HEADS * HEAD_V_PAD
GATE_COL = GLA_HEAD_K
QA_WIDTH = 2 * HEAD_K_PAD
GLA_COL_K = GLA_DK_PAD
GLA_COL_V = GLA_COL_K + GLA_DK_PAD
GLA_COL_R = GLA_COL_V + GLA_DV_PAD
GLA_COL_MQ = GLA_COL_R + GLA_DV_PAD
GLA_IN_PAD = GLA_COL_MQ + XA_WIDTH

SUBLANES = 8
GROUP = SUBLANES * SUBLANES

MIX_TILE = 512
FFN_TILE = 512
FFN_CHUNK = 256
VMEM_LIMIT_BYTES = 56 * 1024 * 1024

F32 = jnp.float32
BF16 = jnp.bfloat16
_NT = (((1,), (1,)), ((), ()))
_TN = (((0,), (0,)), ((), ()))

_dot = functools.partial(jnp.dot, preferred_element_type=F32)
_dot_general = functools.partial(lax.dot_general, preferred_element_type=F32)


def _layer_norm_rows(z, g, b):
    mu = jnp.mean(z, axis=-1, keepdims=True)
    zc = z - mu
    var = jnp.mean(zc * zc, axis=-1, keepdims=True)
    return zc * lax.rsqrt(var + LN_EPS) * g + b


def _deferred_norm_kernel(start_of_row, tile_body, n_tiles, total, n_in, *refs):
    lng_ref, lnb_ref = refs[n_in - 2], refs[n_in - 1]
    out_ref, z_s = refs[n_in], refs[-1]
    i = pl.program_id(0)

    @pl.when(i == 0)
    def _():
        z_s[...] = jnp.zeros_like(z_s)

    @pl.when(i % n_tiles == 0)
    def _():
        start_of_row(*refs)

    @pl.when(i < total)
    def _():
        out_ref[...] = _layer_norm_rows(z_s[...], lng_ref[...], lnb_ref[...])
        tile_body(*refs)

    @pl.when(i == total)
    def _():
        out_ref[...] = _layer_norm_rows(z_s[...], lng_ref[...], lnb_ref[...])


def _build_memory_kv(mem_ref, wkv_ref, kbd_ref, vbd_ref):
    memb = mem_ref[...].astype(BF16)
    kv = _dot(memb, wkv_ref[...])
    k_t = kv[:, :XA_WIDTH].T
    v = kv[:, XA_WIDTH:]
    row_head = lax.broadcasted_iota(jnp.int32, (XA_WIDTH, N_MEM), 0) // XA_HEAD_DIM
    col_head = lax.broadcasted_iota(jnp.int32, (N_MEM, XA_WIDTH), 1) // XA_HEAD_DIM
    for h in range(XA_HEADS):
        kbd_ref[:, h * N_MEM:(h + 1) * N_MEM] = jnp.where(row_head == h, k_t, 0.0).astype(BF16)
        vbd_ref[h * N_MEM:(h + 1) * N_MEM, 0:XA_WIDTH] = jnp.where(col_head == h, v, 0.0).astype(BF16)
        vbd_ref[h * N_MEM:(h + 1) * N_MEM, XA_WIDTH:] = jnp.where(col_head == h, 1.0, 0.0).astype(BF16)


def _memory_xattn(mem_q, kbd_ref, vbd_ref):
    q = (mem_q * (XA_HEAD_DIM ** -0.5)).astype(BF16)
    s = _dot(q, kbd_ref[...])
    es = []
    for h in range(XA_HEADS):
        sh = s[:, h * N_MEM:(h + 1) * N_MEM]
        es.append(jnp.exp(sh - jnp.max(sh, axis=-1, keepdims=True)).astype(BF16))
    acc = _dot(jnp.concatenate(es, axis=-1), vbd_ref[...])
    return acc[:, :XA_WIDTH] / acc[:, XA_WIDTH:]


def _stored_position(shape, axis):
    r = lax.broadcasted_iota(jnp.int32, shape, axis)
    return (r % SUBLANES) * SUBLANES + r // SUBLANES


def _causal_taps(load, tile, prev6, prev7):
    first_sublane = lax.broadcasted_iota(jnp.int32, prev7.shape, 0) == 0
    roll6, roll7 = pltpu.roll(prev6, 1, axis=0), pltpu.roll(prev7, 1, axis=0)
    one_back, two_back = [], []
    for g in range(tile // GROUP):
        base = g * GROUP
        slab6 = load(base + GROUP - 2 * SUBLANES, SUBLANES)
        slab7 = load(base + GROUP - SUBLANES, SUBLANES)
        new6, new7 = pltpu.roll(slab6, 1, axis=0), pltpu.roll(slab7, 1, axis=0)
        back7 = jnp.where(first_sublane, roll7, new7)
        back6 = jnp.where(first_sublane, roll6, new6)
        one_back += [back7, load(base, GROUP - SUBLANES)]
        two_back += [back6, back7, load(base, GROUP - 2 * SUBLANES)]
        roll6, roll7 = new6, new7
    return jnp.concatenate(one_back, axis=0), jnp.concatenate(two_back, axis=0), slab6, slab7


def _gla_start_of_row(x_ref, mem_ref, w_in_ref, wa2_ref, ba_ref, hg_ref, wkv_ref, wo_ref, lng_ref, lnb_ref,
                      out_ref,
                      kbd_ref, vbd_ref, state_ref, q_s, k_s, r_s, la_s, qa_s, rhs_s, kv_s, dc_s, ki_s, ke_s, dec_s,
                      o_s, z_s):
    _build_memory_kv(mem_ref, wkv_ref, kbd_ref, vbd_ref)
    state_ref[...] = jnp.zeros_like(state_ref)
    pad = QA_WIDTH - HEAD_K_PAD - GLA_CHUNK
    for h in range(GLA_HEADS):
        qa_s[:, (h + 1) * QA_WIDTH - pad:(h + 1) * QA_WIDTH] = jnp.zeros((qa_s.shape[0], pad), BF16)
    rhs_s[:, QA_WIDTH - pad:, :] = jnp.zeros((rhs_s.shape[0], pad, HEAD_V_PAD), BF16)


def _gla_tile_body(x_ref, mem_ref, w_in_ref, wa2_ref, ba_ref, hg_ref, wkv_ref, wo_ref, lng_ref, lnb_ref,
                   out_ref,
                   kbd_ref, vbd_ref, state_ref, q_s, k_s, r_s, la_s, qa_s, rhs_s, kv_s, dc_s, ki_s, ke_s, dec_s,
                      o_s, z_s):
    tile = x_ref.shape[0]
    n_chunks = tile // GLA_CHUNK

    x = x_ref[...]
    xb = x.astype(BF16)
    q_raw = _dot(xb, w_in_ref[:, 0:GLA_COL_K])
    q_s[...] = q_raw * (GLA_HEAD_K ** -0.5)
    z = _dot(q_raw[:, 0:HEAD_K_PAD].astype(BF16), wa2_ref[...]) + ba_ref[...]
    la_s[...] = jax.nn.log_sigmoid(z) * (1.0 / GLA_GATE_TAU)
    k_s[...] = _dot(xb, w_in_ref[:, GLA_COL_K:GLA_COL_V])
    v = _dot(xb, w_in_ref[:, GLA_COL_V:GLA_COL_R]).astype(BF16)
    for c in range(n_chunks):
        for h in range(GLA_HEADS):
            rhs_s[c * GLA_HEADS + h, HEAD_K_PAD:HEAD_K_PAD + GLA_CHUNK, :] = (
                v[c * GLA_CHUNK:(c + 1) * GLA_CHUNK, h * HEAD_V_PAD:(h + 1) * HEAD_V_PAD])
    r = _dot(xb, w_in_ref[:, GLA_COL_R:GLA_COL_MQ])
    r_s[...] = r * jax.nn.sigmoid(r)

    causal = (_stored_position((GLA_CHUNK, GLA_CHUNK), 0) >= _stored_position((GLA_CHUNK, GLA_CHUNK), 1))
    tril_ones = causal.astype(BF16)

    for c in range(n_chunks):
        rows = slice(c * GLA_CHUNK, (c + 1) * GLA_CHUNK)
        la = la_s[rows, :]
        la1 = la.astype(BF16)
        rem = la - la1.astype(F32)
        la2 = rem.astype(BF16)
        la3 = (rem - la2.astype(F32)).astype(BF16)
        la_s[rows, :] = _dot(tril_ones, la1) + _dot(tril_ones, la2) + _dot(tril_ones, la3)
    g_cum = la_s[...]
    g_last = [la_s[(c + 1) * GLA_CHUNK - 1:(c + 1) * GLA_CHUNK, :] for c in range(n_chunks)]
    g_last_rows = jnp.concatenate([jnp.broadcast_to(g, (GLA_CHUNK, GLA_DK_PAD)) for g in g_last], axis=0)
    q_dec = (q_s[...] * jnp.exp(g_cum)).astype(BF16)
    for h in range(GLA_HEADS):
        qa_s[:, h * QA_WIDTH:h * QA_WIDTH + HEAD_K_PAD] = q_dec[:, h * HEAD_K_PAD:(h + 1) * HEAD_K_PAD]
    kk = k_s[...]
    ki_s[...] = (kk * jnp.exp(-g_cum)).astype(BF16)
    ke_s[...] = (kk * jnp.exp(g_last_rows - g_cum)).astype(BF16)
    dec_s[...] = jnp.exp(jnp.concatenate(g_last, axis=0))
    for c in range(n_chunks):
        rows = slice(c * GLA_CHUNK, (c + 1) * GLA_CHUNK)
        for h in range(GLA_HEADS):
            att = _dot_general(qa_s[rows, h * QA_WIDTH:h * QA_WIDTH + HEAD_K_PAD],
                               ki_s[rows, h * HEAD_K_PAD:(h + 1) * HEAD_K_PAD], _NT)
            qa_s[rows, h * QA_WIDTH + HEAD_K_PAD:h * QA_WIDTH + HEAD_K_PAD + GLA_CHUNK] = (
                jnp.where(causal, att, 0.0).astype(BF16))
    for c in range(n_chunks):
        rows = slice(c * GLA_CHUNK, (c + 1) * GLA_CHUNK)
        for h in range(GLA_HEADS):
            ks = slice(h * HEAD_K_PAD, (h + 1) * HEAD_K_PAD)
            ch = c * GLA_HEADS + h
            vh = rhs_s[ch, HEAD_K_PAD:HEAD_K_PAD + GLA_CHUNK, :]
            kv_s[ch] = _dot_general(ke_s[rows, ks], vh, _TN)
            dc_s[ch] = jnp.broadcast_to(dec_s[c:c + 1, ks], (HEAD_K_PAD, HEAD_K_PAD)).T

    for c in range(n_chunks):
        for h in range(GLA_HEADS):
            ch = c * GLA_HEADS + h
            st = state_ref[h]
            rhs_s[ch, 0:HEAD_K_PAD, :] = st.astype(BF16)
            dcol = dc_s[ch]
            state_ref[h] = st * jnp.concatenate([dcol, dcol], axis=1) + kv_s[ch]

    for c in range(n_chunks):
        rows = slice(c * GLA_CHUNK, (c + 1) * GLA_CHUNK)
        for h in range(GLA_HEADS):
            vs = slice(h * HEAD_V_PAD, (h + 1) * HEAD_V_PAD)
            o = _dot(qa_s[rows, h * QA_WIDTH:(h + 1) * QA_WIDTH], rhs_s[c * GLA_HEADS + h])
            ms = jnp.sum(o * o, axis=-1, keepdims=True) * (1.0 / GLA_HEAD_V)
            o = o * lax.rsqrt(ms + LN_EPS) * hg_ref[:, vs]
            o_s[rows, vs] = (o * r_s[rows, vs]).astype(BF16)

    mem_q = _dot(xb, w_in_ref[:, GLA_COL_MQ:GLA_IN_PAD])
    xa = _memory_xattn(mem_q, kbd_ref, vbd_ref)
    y_mix = _dot(o_s[...], wo_ref[0:GLA_DV_PAD, :])
    y_xa = _dot(xa.astype(BF16), wo_ref[GLA_DV_PAD:, :])
    z_s[...] = DEEPNORM_ALPHA * x + (y_mix + y_xa)


def _conv_start_of_row(x_ref, mem_ref, w_in_ref, cw_ref, wkv_ref, wo_ref, lng_ref, lnb_ref,
                       out_ref,
                       kbd_ref, vbd_ref, ch_s, tail_ref, z_s):
    _build_memory_kv(mem_ref, wkv_ref, kbd_ref, vbd_ref)
    tail_ref[...] = jnp.zeros_like(tail_ref)


def _conv_tile_body(x_ref, mem_ref, w_in_ref, cw_ref, wkv_ref, wo_ref, lng_ref, lnb_ref,
                    out_ref,
                    kbd_ref, vbd_ref, ch_s, tail_ref, z_s):
    tile = x_ref.shape[0]
    x = x_ref[...]
    xb = x.astype(BF16)
    c_gate = _dot(xb, w_in_ref[:, MIX_WIDTH:2 * MIX_WIDTH])
    h_in = _dot(xb, w_in_ref[:, 2 * MIX_WIDTH:3 * MIX_WIDTH])
    ch_s[...] = c_gate * h_in
    one_back, two_back, last6, last7 = _causal_taps(
        lambda r0, n: ch_s[r0:r0 + n, :], tile, tail_ref[0:SUBLANES, :], tail_ref[SUBLANES:, :])
    tail_ref[0:SUBLANES, :] = last6
    tail_ref[SUBLANES:, :] = last7
    conv = two_back * cw_ref[0:1, :] + one_back * cw_ref[1:2, :] + ch_s[...] * cw_ref[2:3, :]
    b_gate = _dot(xb, w_in_ref[:, 0:MIX_WIDTH])
    mix = (b_gate * conv).astype(BF16)

    mem_q = _dot(xb, w_in_ref[:, 3 * MIX_WIDTH:])
    xa = _memory_xattn(mem_q, kbd_ref, vbd_ref)
    y_mix = _dot(mix, wo_ref[0:MIX_WIDTH, :])
    y_xa = _dot(xa.astype(BF16), wo_ref[MIX_WIDTH:, :])
    z_s[...] = DEEPNORM_ALPHA * x + (y_mix + y_xa)


def _ffn_start_of_row(x_ref, wup_ref, cw_ref, cb_ref, wdn_ref, lng_ref, lnb_ref,
                      out_ref,
                      tail_ref, u_s, h_s, z_s):
    tail_ref[...] = jnp.zeros_like(tail_ref)


def _ffn_tile_body(x_ref, wup_ref, cw_ref, cb_ref, wdn_ref, lng_ref, lnb_ref,
                   out_ref,
                   tail_ref, u_s, h_s, z_s):
    tile = x_ref.shape[0]
    x = x_ref[...]
    xb = x.astype(BF16)

    def conv_cols(half, j):
        cols = slice(half * D_FF + j * FFN_CHUNK, half * D_FF + (j + 1) * FFN_CHUNK)
        buf = 2 * (j % 2) + half
        u_s[buf] = _dot(xb, wup_ref[:, cols])
        one_back, two_back, last6, last7 = _causal_taps(
            lambda r0, n: u_s[buf, r0:r0 + n, :], tile, tail_ref[0:SUBLANES, cols], tail_ref[SUBLANES:, cols])
        tail_ref[0:SUBLANES, cols] = last6
        tail_ref[SUBLANES:, cols] = last7
        return (two_back * cw_ref[0:1, cols] + one_back * cw_ref[1:2, cols]
                + u_s[buf] * cw_ref[2:3, cols] + cb_ref[:, cols])

    for j in range(D_FF // FFN_CHUNK):
        gate = conv_cols(0, j)
        val = conv_cols(1, j)
        h_s[:, j * FFN_CHUNK:(j + 1) * FFN_CHUNK] = (gate * jax.nn.sigmoid(gate) * val).astype(BF16)

    z_s[...] = DEEPNORM_ALPHA * x + _dot(h_s[...], wdn_ref[...])


def _layer_spec(arr, layer):
    zeros = (0,) * (arr.ndim - 1)
    return pl.BlockSpec((None,) + arr.shape[1:], lambda i: (layer,) + zeros, pipeline_mode=pl.Buffered(1))


def _layer_call(start_of_row, tile_body, name, x, row_inputs, layer_inputs, tile, scratch_shapes):
    bsz, seq, _ = x.shape
    n_tiles = seq // tile
    total = bsz * n_tiles

    def tile_of(i):
        t = jnp.minimum(i, total - 1)
        return t // n_tiles, t % n_tiles

    def x_map(i):
        b, s = tile_of(i)
        return b, s, 0

    def out_map(i):
        b, s = tile_of(jnp.maximum(i - 1, 0))
        return b, s, 0

    def row_map(i):
        return tile_of(i)[0], 0, 0

    in_specs = ([pl.BlockSpec((None, tile, D_MODEL), x_map)]
                + [pl.BlockSpec((None,) + a.shape[1:], row_map) for a in row_inputs]
                + [_layer_spec(a, layer) for a, layer in layer_inputs])
    n_in = len(in_specs)
    return pl.pallas_call(
        functools.partial(_deferred_norm_kernel, start_of_row, tile_body, n_tiles, total, n_in),
        grid=(total + 1,),
        in_specs=in_specs,
        out_specs=pl.BlockSpec((None, tile, D_MODEL), out_map),
        out_shape=jax.ShapeDtypeStruct(x.shape, F32),
        scratch_shapes=scratch_shapes + [pltpu.VMEM((tile, D_MODEL), F32)],
        compiler_params=pltpu.CompilerParams(dimension_semantics=("arbitrary",),
                                             vmem_limit_bytes=VMEM_LIMIT_BYTES),
        name=name,
    )(x, *row_inputs, *[a for a, _ in layer_inputs])


def _xattn_scratch():
    return [pltpu.VMEM((XA_WIDTH, XA_HEADS * N_MEM), BF16),
            pltpu.VMEM((XA_HEADS * N_MEM, 2 * XA_WIDTH), BF16)]


def _gla_layer(x, mem, layer_inputs):
    tile = min(MIX_TILE, x.shape[1])
    n_ch = tile // GLA_CHUNK
    scratch = _xattn_scratch() + [
        pltpu.VMEM((GLA_HEADS, HEAD_K_PAD, HEAD_V_PAD), F32),
        pltpu.VMEM((tile, GLA_DK_PAD), F32),
        pltpu.VMEM((tile, GLA_DK_PAD), F32),
        pltpu.VMEM((tile, GLA_DV_PAD), F32),
        pltpu.VMEM((tile, GLA_DK_PAD), F32),
        pltpu.VMEM((tile, GLA_HEADS * QA_WIDTH), BF16),
        pltpu.VMEM((n_ch * GLA_HEADS, QA_WIDTH, HEAD_V_PAD), BF16),
        pltpu.VMEM((n_ch * GLA_HEADS, HEAD_K_PAD, HEAD_V_PAD), F32),
        pltpu.VMEM((n_ch * GLA_HEADS, HEAD_K_PAD, HEAD_K_PAD), F32),
        pltpu.VMEM((tile, GLA_DK_PAD), BF16),
        pltpu.VMEM((tile, GLA_DK_PAD), BF16),
        pltpu.VMEM((n_ch, GLA_DK_PAD), F32),
        pltpu.VMEM((tile, GLA_DV_PAD), BF16),
    ]
    return _layer_call(_gla_start_of_row, _gla_tile_body, "gla_layer", x, [mem], layer_inputs, tile, scratch)


def _conv_layer(x, mem, layer_inputs):
    tile = min(MIX_TILE, x.shape[1])
    scratch = _xattn_scratch() + [
        pltpu.VMEM((tile, MIX_WIDTH), F32),
        pltpu.VMEM((2 * SUBLANES, MIX_WIDTH), F32),
    ]
    return _layer_call(_conv_start_of_row, _conv_tile_body, "conv_layer", x, [mem], layer_inputs, tile, scratch)


def _ffn_layer(x, layer_inputs):
    tile = min(FFN_TILE, x.shape[1])
    scratch = [
        pltpu.VMEM((2 * SUBLANES, 2 * D_FF), F32),
        pltpu.VMEM((4, tile, FFN_CHUNK), F32),
        pltpu.VMEM((tile, D_FF), BF16),
    ]
    return _layer_call(_ffn_start_of_row, _ffn_tile_body, "conv_ffn", x, [], layer_inputs, tile, scratch)


def _pad_heads(w, head_width, padded_width, axis=-1):
    axis = axis % w.ndim
    shape = w.shape
    w = w.reshape(shape[:axis] + (GLA_HEADS, head_width) + shape[axis + 1:])
    pads = [(0, 0)] * w.ndim
    pads[axis + 1] = (0, padded_width - head_width)
    return jnp.pad(w, pads).reshape(shape[:axis] + (GLA_HEADS * padded_width,) + shape[axis + 1:])


def _rows(p):
    return p.reshape(p.shape[0], 1, p.shape[1])


def _to_stored_order(t):
    bsz, seq, d = t.shape
    return t.reshape(bsz, seq // GROUP, SUBLANES, SUBLANES, d).swapaxes(2, 3).reshape(bsz, seq, d)


def _pack_gla_weights(gla_w_in, gla_w_a2, gla_b_a, gla_head_g, gla_w_out):
    n_layers = gla_w_in.shape[0]
    o_k, o_v, o_r = GLA_DK, 2 * GLA_DK, 2 * GLA_DK + GLA_DV
    o_g = o_r + GLA_DV
    o_mq = o_g + GLA_GATE_RANK
    q = gla_w_in[..., :o_k].reshape(n_layers, D_MODEL, GLA_HEADS, GLA_HEAD_K)
    q_pad = jnp.pad(gla_w_in[..., o_g:o_mq][:, :, None, :],
                    ((0, 0), (0, 0), (0, GLA_HEADS - 1), (0, HEAD_K_PAD - GLA_HEAD_K - GLA_GATE_RANK)))
    wq = jnp.concatenate([q, q_pad], axis=-1).reshape(n_layers, D_MODEL, GLA_DK_PAD)
    w_in = jnp.concatenate([wq,
                            _pad_heads(gla_w_in[..., o_k:o_v], GLA_HEAD_K, HEAD_K_PAD),
                            _pad_heads(gla_w_in[..., o_v:o_r], GLA_HEAD_V, HEAD_V_PAD),
                            _pad_heads(gla_w_in[..., o_r:o_g], GLA_HEAD_V, HEAD_V_PAD),
                            gla_w_in[..., o_mq:]], axis=-1).astype(BF16)
    wa2 = jnp.pad(_pad_heads(gla_w_a2, GLA_HEAD_K, HEAD_K_PAD),
                  ((0, 0), (GATE_COL, HEAD_K_PAD - GATE_COL - GLA_GATE_RANK), (0, 0))).astype(BF16)
    ba = _rows(_pad_heads(gla_b_a, GLA_HEAD_K, HEAD_K_PAD))
    hg = _rows(_pad_heads(gla_head_g, GLA_HEAD_V, HEAD_V_PAD))
    wo = jnp.concatenate([_pad_heads(gla_w_out[:, :MIX_WIDTH, :], GLA_HEAD_V, HEAD_V_PAD, axis=1),
                          gla_w_out[:, MIX_WIDTH:, :]], axis=1).astype(BF16)
    return w_in, wa2, ba, hg, wo


def kernel(x, mem, gla_w_in, gla_w_a2, gla_b_a, gla_head_g, gla_w_out, conv_w_in, conv_w, conv_w_out,
           w_mem_kv, ln1_g, ln1_b, ffn_w_up, ffn_conv_w, ffn_conv_b, ffn_w_down, ln2_g, ln2_b):
    gla_in, gla_a2, gla_ba, gla_hg, gla_wo = _pack_gla_weights(gla_w_in, gla_w_a2, gla_b_a, gla_head_g, gla_w_out)
    conv_in, conv_wo = conv_w_in.astype(BF16), conv_w_out.astype(BF16)
    wkv = w_mem_kv.astype(BF16)
    w_up, w_down = ffn_w_up.astype(BF16), ffn_w_down.astype(BF16)
    ln1g, ln1b, ln2g, ln2b, ffn_cb = _rows(ln1_g), _rows(ln1_b), _rows(ln2_g), _rows(ln2_b), _rows(ffn_conv_b)

    x = _to_stored_order(x)
    for i in range(DEPTH):
        j = i // 2
        if i % 2 == 0:
            x = _gla_layer(x, mem, [(gla_in, j), (gla_a2, j), (gla_ba, j), (gla_hg, j), (wkv, i), (gla_wo, j),
                                    (ln1g, i), (ln1b, i)])
        else:
            x = _conv_layer(x, mem, [(conv_in, j), (conv_w, j), (wkv, i), (conv_wo, j), (ln1g, i), (ln1b, i)])
        x = _ffn_layer(x, [(w_up, i), (ffn_conv_w, i), (ffn_cb, i), (w_down, i), (ln2g, i), (ln2b, i)])
    return _to_stored_order(x)
```

```python
import functools
import math

import jax
import jax.numpy as jnp
from jax import lax
from jax.experimental import pallas as pl
from jax.experimental.pallas import tpu as pltpu

D_MODEL = 1024
DEPTH = 4
N_MEM = 256
XA_HEADS = 4
XA_WIDTH = D_MODEL // 4
XA_HEAD_DIM = XA_WIDTH // XA_HEADS
MIX_WIDTH = D_MODEL - XA_WIDTH
GLA_HEADS = 4
GLA_DV = MIX_WIDTH
GLA_DK = MIX_WIDTH // 2
GLA_HEAD_K = GLA_DK // GLA_HEADS
GLA_HEAD_V = GLA_DV // GLA_HEADS
GLA_GATE_RANK = 16
GLA_GATE_TAU = 16.0
GLA_CHUNK = 64
CONV_WIDTH = 3
D_FF = int(math.ceil(8 * D_MODEL / 3 / 128)) * 128
LN_EPS = 1e-5
DEEPNORM_ALPHA = (2 * DEPTH) ** 0.25

HEAD_K_PAD = 128
HEAD_V_PAD = 256
GLA_DK_PAD = GLA_HEADS * HEAD_K_PAD
GLA_DV_PAD = GLA_HEADS * HEAD_V_PAD
GATE_COL = GLA_HEAD_K
QA_WIDTH = 2 * HEAD_K_PAD
GLA_COL_K = GLA_DK_PAD
GLA_COL_V = GLA_COL_K + GLA_DK_PAD
GLA_COL_R = GLA_COL_V + GLA_DV_PAD
GLA_COL_MQ = GLA_COL_R + GLA_DV_PAD
GLA_IN_PAD = GLA_COL_MQ + XA_WIDTH

SUBLANES = 8
LANES = 128
GROUP = SUBLANES * SUBLANES

MIX_TILE = 512
FFN_TILE = 512
FFN_CHUNK = 256
VMEM_LIMIT_BYTES = 56 * 1024 * 1024

F32 = jnp.float32
BF16 = jnp.bfloat16
_NT = (((1,), (1,)), ((), ()))
_TN = (((0,), (0,)), ((), ()))

_dot = functools.partial(jnp.dot, preferred_element_type=F32)
_dot_general = functools.partial(lax.dot_general, preferred_element_type=F32)


def _store_slabs(z_s, z):
    for k in range(z_s.shape[0]):
        z_s[k] = z[:, k * LANES:(k + 1) * LANES]


def _norm_from_slabs(z_s, lng_ref, lnb_ref, out_ref, swap_order):
    n_slabs, tile, _ = z_s.shape
    if swap_order:
        def load(k):
            return jnp.concatenate(
                [z_s[k, pl.ds(g * GROUP + v, SUBLANES, stride=SUBLANES), :]
                 for g in range(tile // GROUP) for v in range(SUBLANES)], axis=0)
    else:
        def load(k):
            return z_s[k]
    zs = [load(k) for k in range(n_slabs)]
    mu = jnp.sum(functools.reduce(jnp.add, zs), axis=-1, keepdims=True) * (1.0 / D_MODEL)
    cs = [z - mu for z in zs]
    var = jnp.sum(functools.reduce(jnp.add, [c * c for c in cs]), axis=-1, keepdims=True) * (1.0 / D_MODEL)
    rstd = lax.rsqrt(var + LN_EPS)
    for k in range(n_slabs):
        cols = slice(k * LANES, (k + 1) * LANES)
        out_ref[:, cols] = cs[k] * rstd * lng_ref[:, cols] + lnb_ref[:, cols]


def _deferred_norm_kernel(start_of_row, tile_body, n_tiles, total, n_in, swap_order, *refs):
    lng_ref, lnb_ref = refs[n_in - 2], refs[n_in - 1]
    out_ref, z_s = refs[n_in], refs[-1]
    i = pl.program_id(0)

    @pl.when(i == 0)
    def _():
        z_s[...] = jnp.zeros_like(z_s)

    @pl.when(i % n_tiles == 0)
    def _():
        start_of_row(*refs)

    @pl.when(i < total)
    def _():
        _norm_from_slabs(z_s, lng_ref, lnb_ref, out_ref, swap_order)
        tile_body(*refs)

    @pl.when(i == total)
    def _():
        _norm_from_slabs(z_s, lng_ref, lnb_ref, out_ref, swap_order)


def _build_memory_kv(mem_ref, wkv_ref, kbd_ref, vbd_ref):
    memb = mem_ref[...].astype(BF16)
    kv = _dot(memb, wkv_ref[...])
    k_t = kv[:, :XA_WIDTH].T
    v = kv[:, XA_WIDTH:]
    row_head = lax.broadcasted_iota(jnp.int32, (XA_WIDTH, N_MEM), 0) // XA_HEAD_DIM
    col_head = lax.broadcasted_iota(jnp.int32, (N_MEM, XA_WIDTH), 1) // XA_HEAD_DIM
    for h in range(XA_HEADS):
        kbd_ref[:, h * N_MEM:(h + 1) * N_MEM] = jnp.where(row_head == h, k_t, 0.0).astype(BF16)
        vbd_ref[h * N_MEM:(h + 1) * N_MEM, 0:XA_WIDTH] = jnp.where(col_head == h, v, 0.0).astype(BF16)
        vbd_ref[h * N_MEM:(h + 1) * N_MEM, XA_WIDTH:] = jnp.where(col_head == h, 1.0, 0.0).astype(BF16)


def _memory_xattn(mem_q, kbd_ref, vbd_ref):
    q = (mem_q * (XA_HEAD_DIM ** -0.5)).astype(BF16)
    s = _dot(q, kbd_ref[...])
    es = []
    for h in range(XA_HEADS):
        sh = s[:, h * N_MEM:(h + 1) * N_MEM]
        es.append(jnp.exp(sh - jnp.max(sh, axis=-1, keepdims=True)).astype(BF16))
    acc = _dot(jnp.concatenate(es, axis=-1), vbd_ref[...])
    return acc[:, :XA_WIDTH] / acc[:, XA_WIDTH:]


def _stored_position(shape, axis):
    r = lax.broadcasted_iota(jnp.int32, shape, axis)
    return (r % SUBLANES) * SUBLANES + r // SUBLANES


def _causal_taps(load, tile, prev6, prev7):
    first_sublane = lax.broadcasted_iota(jnp.int32, prev7.shape, 0) == 0
    roll6, roll7 = pltpu.roll(prev6, 1, axis=0), pltpu.roll(prev7, 1, axis=0)
    one_back, two_back = [], []
    for g in range(tile // GROUP):
        base = g * GROUP
        slab6 = load(base + GROUP - 2 * SUBLANES, SUBLANES)
        slab7 = load(base + GROUP - SUBLANES, SUBLANES)
        new6, new7 = pltpu.roll(slab6, 1, axis=0), pltpu.roll(slab7, 1, axis=0)
        back7 = jnp.where(first_sublane, roll7, new7)
        back6 = jnp.where(first_sublane, roll6, new6)
        one_back += [back7, load(base, GROUP - SUBLANES)]
        two_back += [back6, back7, load(base, GROUP - 2 * SUBLANES)]
        roll6, roll7 = new6, new7
    return jnp.concatenate(one_back, axis=0), jnp.concatenate(two_back, axis=0), slab6, slab7


def _gla_start_of_row(x_ref, mem_ref, w_in_ref, wa2_ref, ba_ref, hg_ref, wkv_ref, wo_ref, lng_ref, lnb_ref,
                      out_ref,
                      kbd_ref, vbd_ref, state_ref, q_s, k_s, r_s, la_s, qa_s, rhs_s, kv_s, dc_s, ki_s, ke_s, dec_s,
                      o_s, z_s):
    _build_memory_kv(mem_ref, wkv_ref, kbd_ref, vbd_ref)
    state_ref[...] = jnp.zeros_like(state_ref)
    pad = QA_WIDTH - HEAD_K_PAD - GLA_CHUNK
    for h in range(GLA_HEADS):
        qa_s[:, (h + 1) * QA_WIDTH - pad:(h + 1) * QA_WIDTH] = jnp.zeros((qa_s.shape[0], pad), BF16)
    rhs_s[:, QA_WIDTH - pad:, :] = jnp.zeros((rhs_s.shape[0], pad, HEAD_V_PAD), BF16)


def _gla_tile_body(natural_order, x_ref, mem_ref, w_in_ref, wa2_ref, ba_ref, hg_ref, wkv_ref, wo_ref, lng_ref, lnb_ref,
                   out_ref,
                   kbd_ref, vbd_ref, state_ref, q_s, k_s, r_s, la_s, qa_s, rhs_s, kv_s, dc_s, ki_s, ke_s, dec_s,
                      o_s, z_s):
    tile = x_ref.shape[0]
    n_chunks = tile // GLA_CHUNK

    x = x_ref[...]
    xb = x.astype(BF16)
    q_raw = _dot(xb, w_in_ref[:, 0:GLA_COL_K])
    q_s[...] = q_raw * (GLA_HEAD_K ** -0.5)
    z = _dot(q_raw[:, 0:HEAD_K_PAD].astype(BF16), wa2_ref[...]) + ba_ref[...]
    la_s[...] = jax.nn.log_sigmoid(z) * (1.0 / GLA_GATE_TAU)
    k_s[...] = _dot(xb, w_in_ref[:, GLA_COL_K:GLA_COL_V])
    v = _dot(xb, w_in_ref[:, GLA_COL_V:GLA_COL_R]).astype(BF16)
    for c in range(n_chunks):
        for h in range(GLA_HEADS):
            rhs_s[c * GLA_HEADS + h, HEAD_K_PAD:HEAD_K_PAD + GLA_CHUNK, :] = (
                v[c * GLA_CHUNK:(c + 1) * GLA_CHUNK, h * HEAD_V_PAD:(h + 1) * HEAD_V_PAD])
    r = _dot(xb, w_in_ref[:, GLA_COL_R:GLA_COL_MQ])
    r_s[...] = r * jax.nn.sigmoid(r)

    if natural_order:
        causal = (lax.broadcasted_iota(jnp.int32, (GLA_CHUNK, GLA_CHUNK), 0)
                  >= lax.broadcasted_iota(jnp.int32, (GLA_CHUNK, GLA_CHUNK), 1))
    else:
        causal = (_stored_position((GLA_CHUNK, GLA_CHUNK), 0) >= _stored_position((GLA_CHUNK, GLA_CHUNK), 1))
    tril_ones = causal.astype(BF16)

    for c in range(n_chunks):
        rows = slice(c * GLA_CHUNK, (c + 1) * GLA_CHUNK)
        la = la_s[rows, :]
        la1 = la.astype(BF16)
        rem = la - la1.astype(F32)
        la2 = rem.astype(BF16)
        la3 = (rem - la2.astype(F32)).astype(BF16)
        la_s[rows, :] = _dot(tril_ones, la1) + _dot(tril_ones, la2) + _dot(tril_ones, la3)
    g_cum = la_s[...]
    g_last = [la_s[(c + 1) * GLA_CHUNK - 1:(c + 1) * GLA_CHUNK, :] for c in range(n_chunks)]
    g_last_rows = jnp.concatenate([jnp.broadcast_to(g, (GLA_CHUNK, GLA_DK_PAD)) for g in g_last], axis=0)
    q_dec = (q_s[...] * jnp.exp(g_cum)).astype(BF16)
    for h in range(GLA_HEADS):
        qa_s[:, h * QA_WIDTH:h * QA_WIDTH + HEAD_K_PAD] = q_dec[:, h * HEAD_K_PAD:(h + 1) * HEAD_K_PAD]
    kk = k_s[...]
    ki_s[...] = (kk * jnp.exp(-g_cum)).astype(BF16)
    ke_s[...] = (kk * jnp.exp(g_last_rows - g_cum)).astype(BF16)
    dec_s[...] = jnp.exp(jnp.concatenate(g_last, axis=0))
    for c in range(n_chunks):
        rows = slice(c * GLA_CHUNK, (c + 1) * GLA_CHUNK)
        for h in range(GLA_HEADS):
            att = _dot_general(qa_s[rows, h * QA_WIDTH:h * QA_WIDTH + HEAD_K_PAD],
                               ki_s[rows, h * HEAD_K_PAD:(h + 1) * HEAD_K_PAD], _NT)
            qa_s[rows, h * QA_WIDTH + HEAD_K_PAD:h * QA_WIDTH + HEAD_K_PAD + GLA_CHUNK] = (
                jnp.where(causal, att, 0.0).astype(BF16))
    for c in range(n_chunks):
        rows = slice(c * GLA_CHUNK, (c + 1) * GLA_CHUNK)
        for h in range(GLA_HEADS):
            ks = slice(h * HEAD_K_PAD, (h + 1) * HEAD_K_PAD)
            ch = c * GLA_HEADS + h
            vh = rhs_s[ch, HEAD_K_PAD:HEAD_K_PAD + GLA_CHUNK, :]
            kv_s[ch] = _dot_general(ke_s[rows, ks], vh, _TN)
            dc_s[ch] = jnp.broadcast_to(dec_s[c:c + 1, ks], (HEAD_K_PAD, HEAD_K_PAD)).T

    for c in range(n_chunks):
        for h in range(GLA_HEADS):
            ch = c * GLA_HEADS + h
            st = state_ref[h]
            rhs_s[ch, 0:HEAD_K_PAD, :] = st.astype(BF16)
            dcol = dc_s[ch]
            state_ref[h] = st * jnp.concatenate([dcol, dcol], axis=1) + kv_s[ch]

    for c in range(n_chunks):
        rows = slice(c * GLA_CHUNK, (c + 1) * GLA_CHUNK)
        for h in range(GLA_HEADS):
            vs = slice(h * HEAD_V_PAD, (h + 1) * HEAD_V_PAD)
            o = _dot(qa_s[rows, h * QA_WIDTH:(h + 1) * QA_WIDTH], rhs_s[c * GLA_HEADS + h])
            ms = jnp.sum(o * o, axis=-1, keepdims=True) * (1.0 / GLA_HEAD_V)
            o = o * lax.rsqrt(ms + LN_EPS) * hg_ref[:, vs]
            o_s[rows, vs] = (o * r_s[rows, vs]).astype(BF16)

    mem_q = _dot(xb, w_in_ref[:, GLA_COL_MQ:GLA_IN_PAD])
    xa = _memory_xattn(mem_q, kbd_ref, vbd_ref)
    y_mix = _dot(o_s[...], wo_ref[0:GLA_DV_PAD, :])
    y_xa = _dot(xa.astype(BF16), wo_ref[GLA_DV_PAD:, :])
    _store_slabs(z_s, DEEPNORM_ALPHA * x + (y_mix + y_xa))


def _conv_start_of_row(x_ref, mem_ref, w_in_ref, cw_ref, wkv_ref, wo_ref, lng_ref, lnb_ref,
                       out_ref,
                       kbd_ref, vbd_ref, ch_s, tail_ref, z_s):
    _build_memory_kv(mem_ref, wkv_ref, kbd_ref, vbd_ref)
    tail_ref[...] = jnp.zeros_like(tail_ref)


def _conv_tile_body(x_ref, mem_ref, w_in_ref, cw_ref, wkv_ref, wo_ref, lng_ref, lnb_ref,
                    out_ref,
                    kbd_ref, vbd_ref, ch_s, tail_ref, z_s):
    tile = x_ref.shape[0]
    x = x_ref[...]
    xb = x.astype(BF16)
    c_gate = _dot(xb, w_in_ref[:, MIX_WIDTH:2 * MIX_WIDTH])
    h_in = _dot(xb, w_in_ref[:, 2 * MIX_WIDTH:3 * MIX_WIDTH])
    ch_s[...] = c_gate * h_in
    one_back, two_back, last6, last7 = _causal_taps(
        lambda r0, n: ch_s[r0:r0 + n, :], tile, tail_ref[0:SUBLANES, :], tail_ref[SUBLANES:, :])
    tail_ref[0:SUBLANES, :] = last6
    tail_ref[SUBLANES:, :] = last7
    conv = two_back * cw_ref[0:1, :] + one_back * cw_ref[1:2, :] + ch_s[...] * cw_ref[2:3, :]
    b_gate = _dot(xb, w_in_ref[:, 0:MIX_WIDTH])
    mix = (b_gate * conv).astype(BF16)

    mem_q = _dot(xb, w_in_ref[:, 3 * MIX_WIDTH:])
    xa = _memory_xattn(mem_q, kbd_ref, vbd_ref)
    y_mix = _dot(mix, wo_ref[0:MIX_WIDTH, :])
    y_xa = _dot(xa.astype(BF16), wo_ref[MIX_WIDTH:, :])
    _store_slabs(z_s, DEEPNORM_ALPHA * x + (y_mix + y_xa))


def _ffn_start_of_row(x_ref, wup_ref, cw_ref, cb_ref, wdn_ref, lng_ref, lnb_ref,
                      out_ref,
                      tail_ref, u_s, h_s, z_s):
    tail_ref[...] = jnp.zeros_like(tail_ref)


def _ffn_tile_body(x_ref, wup_ref, cw_ref, cb_ref, wdn_ref, lng_ref, lnb_ref,
                   out_ref,
                   tail_ref, u_s, h_s, z_s):
    tile = x_ref.shape[0]
    x = x_ref[...]
    xb = x.astype(BF16)

    def conv_cols(half, j):
        cols = slice(half * D_FF + j * FFN_CHUNK, half * D_FF + (j + 1) * FFN_CHUNK)
        buf = 2 * (j % 2) + half
        u_s[buf] = _dot(xb, wup_ref[:, cols])
        one_back, two_back, last6, last7 = _causal_taps(
            lambda r0, n: u_s[buf, r0:r0 + n, :], tile, tail_ref[0:SUBLANES, cols], tail_ref[SUBLANES:, cols])
        tail_ref[0:SUBLANES, cols] = last6
        tail_ref[SUBLANES:, cols] = last7
        return (two_back * cw_ref[0:1, cols] + one_back * cw_ref[1:2, cols]
                + u_s[buf] * cw_ref[2:3, cols] + cb_ref[:, cols])

    for j in range(D_FF // FFN_CHUNK):
        gate = conv_cols(0, j)
        val = conv_cols(1, j)
        h_s[:, j * FFN_CHUNK:(j + 1) * FFN_CHUNK] = (gate * jax.nn.sigmoid(gate) * val).astype(BF16)

    _store_slabs(z_s, DEEPNORM_ALPHA * x + _dot(h_s[...], wdn_ref[...]))


def _layer_spec(arr, layer):
    zeros = (0,) * (arr.ndim - 1)
    return pl.BlockSpec((None,) + arr.shape[1:], lambda i: (layer,) + zeros, pipeline_mode=pl.Buffered(1))


def _layer_call(start_of_row, tile_body, name, x, row_inputs, layer_inputs, tile, scratch_shapes,
                swap_order=False):
    bsz, seq, _ = x.shape
    n_tiles = seq // tile
    total = bsz * n_tiles

    def tile_of(i):
        t = jnp.minimum(i, total - 1)
        return t // n_tiles, t % n_tiles

    def x_map(i):
        b, s = tile_of(i)
        return b, s, 0

    def out_map(i):
        b, s = tile_of(jnp.maximum(i - 1, 0))
        return b, s, 0

    def row_map(i):
        return tile_of(i)[0], 0, 0

    in_specs = ([pl.BlockSpec((None, tile, D_MODEL), x_map)]
                + [pl.BlockSpec((None,) + a.shape[1:], row_map) for a in row_inputs]
                + [_layer_spec(a, layer) for a, layer in layer_inputs])
    n_in = len(in_specs)
    return pl.pallas_call(
        functools.partial(_deferred_norm_kernel, start_of_row, tile_body, n_tiles, total, n_in, swap_order),
        grid=(total + 1,),
        in_specs=in_specs,
        out_specs=pl.BlockSpec((None, tile, D_MODEL), out_map),
        out_shape=jax.ShapeDtypeStruct(x.shape, F32),
        scratch_shapes=scratch_shapes + [pltpu.VMEM((D_MODEL // LANES, tile, LANES), F32)],
        compiler_params=pltpu.CompilerParams(dimension_semantics=("arbitrary",),
                                             vmem_limit_bytes=VMEM_LIMIT_BYTES),
        name=name,
    )(x, *row_inputs, *[a for a, _ in layer_inputs])


def _xattn_scratch():
    return [pltpu.VMEM((XA_WIDTH, XA_HEADS * N_MEM), BF16),
            pltpu.VMEM((XA_HEADS * N_MEM, 2 * XA_WIDTH), BF16)]


def _gla_layer(x, mem, layer_inputs, natural_in):
    tile = min(MIX_TILE, x.shape[1])
    n_ch = tile // GLA_CHUNK
    scratch = _xattn_scratch() + [
        pltpu.VMEM((GLA_HEADS, HEAD_K_PAD, HEAD_V_PAD), F32),
        pltpu.VMEM((tile, GLA_DK_PAD), F32),
        pltpu.VMEM((tile, GLA_DK_PAD), F32),
        pltpu.VMEM((tile, GLA_DV_PAD), F32),
        pltpu.VMEM((tile, GLA_DK_PAD), F32),
        pltpu.VMEM((tile, GLA_HEADS * QA_WIDTH), BF16),
        pltpu.VMEM((n_ch * GLA_HEADS, QA_WIDTH, HEAD_V_PAD), BF16),
        pltpu.VMEM((n_ch * GLA_HEADS, HEAD_K_PAD, HEAD_V_PAD), F32),
        pltpu.VMEM((n_ch * GLA_HEADS, HEAD_K_PAD, HEAD_K_PAD), F32),
        pltpu.VMEM((tile, GLA_DK_PAD), BF16),
        pltpu.VMEM((tile, GLA_DK_PAD), BF16),
        pltpu.VMEM((n_ch, GLA_DK_PAD), F32),
        pltpu.VMEM((tile, GLA_DV_PAD), BF16),
    ]
    return _layer_call(_gla_start_of_row, functools.partial(_gla_tile_body, natural_in), "gla_layer", x, [mem],
                       layer_inputs, tile, scratch, swap_order=natural_in)


def _conv_layer(x, mem, layer_inputs):
    tile = min(MIX_TILE, x.shape[1])
    scratch = _xattn_scratch() + [
        pltpu.VMEM((tile, MIX_WIDTH), F32),
        pltpu.VMEM((2 * SUBLANES, MIX_WIDTH), F32),
    ]
    return _layer_call(_conv_start_of_row, _conv_tile_body, "conv_layer", x, [mem], layer_inputs, tile, scratch)


def _ffn_layer(x, layer_inputs, natural_out):
    tile = min(FFN_TILE, x.shape[1])
    scratch = [
        pltpu.VMEM((2 * SUBLANES, 2 * D_FF), F32),
        pltpu.VMEM((4, tile, FFN_CHUNK), F32),
        pltpu.VMEM((tile, D_FF), BF16),
    ]
    return _layer_call(_ffn_start_of_row, _ffn_tile_body, "conv_ffn", x, [], layer_inputs, tile, scratch,
                       swap_order=natural_out)


def _pad_heads(w, head_width, padded_width, axis=-1, first_pad=None):
    axis = axis % w.ndim
    pieces = []
    for h in range(GLA_HEADS):
        pieces.append(lax.slice_in_dim(w, h * head_width, (h + 1) * head_width, axis=axis))
        n_zero = padded_width - head_width
        if h == 0 and first_pad is not None:
            pieces.append(first_pad)
            n_zero -= first_pad.shape[axis]
        pieces.append(jnp.zeros(w.shape[:axis] + (n_zero,) + w.shape[axis + 1:], w.dtype))
    return jnp.concatenate(pieces, axis=axis)


def _rows(p):
    return p.reshape(p.shape[0], 1, p.shape[1])


def _pack_gla_weights(gla_w_in, gla_w_a2, gla_b_a, gla_head_g, gla_w_out):
    o_k, o_v, o_r = GLA_DK, 2 * GLA_DK, 2 * GLA_DK + GLA_DV
    o_g = o_r + GLA_DV
    o_mq = o_g + GLA_GATE_RANK
    wq = _pad_heads(gla_w_in[..., :o_k], GLA_HEAD_K, HEAD_K_PAD, first_pad=gla_w_in[..., o_g:o_mq])
    w_in = jnp.concatenate([wq,
                            _pad_heads(gla_w_in[..., o_k:o_v], GLA_HEAD_K, HEAD_K_PAD),
                            _pad_heads(gla_w_in[..., o_v:o_r], GLA_HEAD_V, HEAD_V_PAD),
                            _pad_heads(gla_w_in[..., o_r:o_g], GLA_HEAD_V, HEAD_V_PAD),
                            gla_w_in[..., o_mq:]], axis=-1).astype(BF16)
    a2 = _pad_heads(gla_w_a2, GLA_HEAD_K, HEAD_K_PAD)
    wa2 = jnp.concatenate([jnp.zeros((a2.shape[0], GATE_COL, GLA_DK_PAD), F32), a2,
                           jnp.zeros((a2.shape[0], HEAD_K_PAD - GATE_COL - GLA_GATE_RANK, GLA_DK_PAD), F32)],
                          axis=1).astype(BF16)
    ba = _rows(_pad_heads(gla_b_a, GLA_HEAD_K, HEAD_K_PAD))
    hg = _rows(_pad_heads(gla_head_g, GLA_HEAD_V, HEAD_V_PAD))
    wo = jnp.concatenate([_pad_heads(gla_w_out[:, :MIX_WIDTH, :], GLA_HEAD_V, HEAD_V_PAD, axis=1),
                          gla_w_out[:, MIX_WIDTH:, :]], axis=1).astype(BF16)
    return w_in, wa2, ba, hg, wo


def kernel(x, mem, gla_w_in, gla_w_a2, gla_b_a, gla_head_g, gla_w_out, conv_w_in, conv_w, conv_w_out,
           w_mem_kv, ln1_g, ln1_b, ffn_w_up, ffn_conv_w, ffn_conv_b, ffn_w_down, ln2_g, ln2_b):
    gla_in, gla_a2, gla_ba, gla_hg, gla_wo = _pack_gla_weights(gla_w_in, gla_w_a2, gla_b_a, gla_head_g, gla_w_out)
    conv_in, conv_wo = conv_w_in.astype(BF16), conv_w_out.astype(BF16)
    wkv = w_mem_kv.astype(BF16)
    w_up, w_down = ffn_w_up.astype(BF16), ffn_w_down.astype(BF16)
    ln1g, ln1b, ln2g, ln2b, ffn_cb = _rows(ln1_g), _rows(ln1_b), _rows(ln2_g), _rows(ln2_b), _rows(ffn_conv_b)

    for i in range(DEPTH):
        j = i // 2
        if i % 2 == 0:
            x = _gla_layer(x, mem, [(gla_in, j), (gla_a2, j), (gla_ba, j), (gla_hg, j), (wkv, i), (gla_wo, j),
                                    (ln1g, i), (ln1b, i)], natural_in=(i == 0))
        else:
            x = _conv_layer(x, mem, [(conv_in, j), (conv_w, j), (wkv, i), (conv_wo, j), (ln1g, i), (ln1b, i)])
        x = _ffn_layer(x, [(w_up, i), (ffn_conv_w, i), (ffn_cb, i), (w_down, i), (ln2g, i), (ln2b, i)],
                       natural_out=(i == DEPTH - 1))
    return x
```

```python
import functools
import math

import jax
import jax.numpy as jnp
from jax import lax
from jax.experimental import pallas as pl
from jax.experimental.pallas import tpu as pltpu

D_MODEL = 1024
DEPTH = 4
N_MEM = 256
XA_HEADS = 4
XA_WIDTH = D_MODEL // 4
XA_HEAD_DIM = XA_WIDTH // XA_HEADS
MIX_WIDTH = D_MODEL - XA_WIDTH
GLA_HEADS = 4
GLA_DV = MIX_WIDTH
GLA_DK = MIX_WIDTH // 2
GLA_HEAD_K = GLA_DK // GLA_HEADS
GLA_HEAD_V = GLA_DV // GLA_HEADS
GLA_GATE_RANK = 16
GLA_GATE_TAU = 16.0
GLA_CHUNK = 64
CONV_WIDTH = 3
D_FF = int(math.ceil(8 * D_MODEL / 3 / 128)) * 128
LN_EPS = 1e-5
DEEPNORM_ALPHA = (2 * DEPTH) ** 0.25

HEAD_K_PAD = 128
HEAD_V_PAD = 256
GLA_DK_PAD = GLA_HEADS * HEAD_K_PAD
GLA_DV_PAD = GLA_HEADS * HEAD_V_PAD
GATE_COL = GLA_HEAD_K
QA_WIDTH = 2 * HEAD_K_PAD
GLA_COL_K = GLA_DK_PAD
GLA_COL_V = GLA_COL_K + GLA_DK_PAD
GLA_COL_R = GLA_COL_V + GLA_DV
GLA_COL_MQ = GLA_COL_R + GLA_DV
GLA_IN_PAD = GLA_COL_MQ + XA_WIDTH

SUBLANES = 8
LANES = 128
GROUP = SUBLANES * SUBLANES

MIX_TILE = 512
FFN_TILE = 512
FFN_CHUNK = 256
VMEM_LIMIT_BYTES = 56 * 1024 * 1024

F32 = jnp.float32
BF16 = jnp.bfloat16
_NT = (((1,), (1,)), ((), ()))
_TN = (((0,), (0,)), ((), ()))

_dot = functools.partial(jnp.dot, preferred_element_type=F32)
_dot_general = functools.partial(lax.dot_general, preferred_element_type=F32)


def _store_slabs(z_s, z):
    for k in range(z_s.shape[0]):
        z_s[k] = z[:, k * LANES:(k + 1) * LANES]


def _norm_from_slabs(z_s, lng_ref, lnb_ref, out_ref, swap_order):
    n_slabs, tile, _ = z_s.shape
    if swap_order:
        def load(k):
            return jnp.concatenate(
                [z_s[k, pl.ds(g * GROUP + v, SUBLANES, stride=SUBLANES), :]
                 for g in range(tile // GROUP) for v in range(SUBLANES)], axis=0)
    else:
        def load(k):
            return z_s[k]
    zs = [load(k) for k in range(n_slabs)]
    mu = jnp.sum(functools.reduce(jnp.add, zs), axis=-1, keepdims=True) * (1.0 / D_MODEL)
    cs = [z - mu for z in zs]
    var = jnp.sum(functools.reduce(jnp.add, [c * c for c in cs]), axis=-1, keepdims=True) * (1.0 / D_MODEL)
    rstd = lax.rsqrt(var + LN_EPS)
    for k in range(n_slabs):
        cols = slice(k * LANES, (k + 1) * LANES)
        out_ref[:, cols] = cs[k] * rstd * lng_ref[:, cols] + lnb_ref[:, cols]


def _deferred_norm_kernel(start_of_row, tile_body, n_tiles, total, n_in, swap_order, *refs):
    lng_ref, lnb_ref = refs[n_in - 2], refs[n_in - 1]
    out_ref, z_s = refs[n_in], refs[-1]
    i = pl.program_id(0)

    @pl.when(i == 0)
    def _():
        z_s[...] = jnp.zeros_like(z_s)

    @pl.when(i % n_tiles == 0)
    def _():
        start_of_row(*refs)

    @pl.when(i < total)
    def _():
        _norm_from_slabs(z_s, lng_ref, lnb_ref, out_ref, swap_order)
        tile_body(*refs)

    @pl.when(i == total)
    def _():
        _norm_from_slabs(z_s, lng_ref, lnb_ref, out_ref, swap_order)


def _build_memory_kv(mem_ref, wkv_ref, kbd_ref, vbd_ref):
    memb = mem_ref[...].astype(BF16)
    kv = _dot(memb, wkv_ref[...])
    k_t = kv[:, :XA_WIDTH].T
    v = kv[:, XA_WIDTH:]
    row_head = lax.broadcasted_iota(jnp.int32, (XA_WIDTH, N_MEM), 0) // XA_HEAD_DIM
    col_head = lax.broadcasted_iota(jnp.int32, (N_MEM, XA_WIDTH), 1) // XA_HEAD_DIM
    for h in range(XA_HEADS):
        kbd_ref[:, h * N_MEM:(h + 1) * N_MEM] = jnp.where(row_head == h, k_t, 0.0).astype(BF16)
        vbd_ref[h * N_MEM:(h + 1) * N_MEM, 0:XA_WIDTH] = jnp.where(col_head == h, v, 0.0).astype(BF16)
        vbd_ref[h * N_MEM:(h + 1) * N_MEM, XA_WIDTH:] = jnp.where(col_head == h, 1.0, 0.0).astype(BF16)


def _memory_xattn(mem_q, kbd_ref, vbd_ref):
    q = (mem_q * (XA_HEAD_DIM ** -0.5)).astype(BF16)
    s = _dot(q, kbd_ref[...])
    es = []
    for h in range(XA_HEADS):
        sh = s[:, h * N_MEM:(h + 1) * N_MEM]
        es.append(jnp.exp(sh - jnp.max(sh, axis=-1, keepdims=True)).astype(BF16))
    acc = _dot(jnp.concatenate(es, axis=-1), vbd_ref[...])
    return acc[:, :XA_WIDTH] / acc[:, XA_WIDTH:]


def _stored_position(shape, axis):
    r = lax.broadcasted_iota(jnp.int32, shape, axis)
    return (r % SUBLANES) * SUBLANES + r // SUBLANES


def _causal_taps(load, tile, prev6, prev7):
    first_sublane = lax.broadcasted_iota(jnp.int32, prev7.shape, 0) == 0
    roll6, roll7 = pltpu.roll(prev6, 1, axis=0), pltpu.roll(prev7, 1, axis=0)
    one_back, two_back = [], []
    for g in range(tile // GROUP):
        base = g * GROUP
        slab6 = load(base + GROUP - 2 * SUBLANES, SUBLANES)
        slab7 = load(base + GROUP - SUBLANES, SUBLANES)
        new6, new7 = pltpu.roll(slab6, 1, axis=0), pltpu.roll(slab7, 1, axis=0)
        back7 = jnp.where(first_sublane, roll7, new7)
        back6 = jnp.where(first_sublane, roll6, new6)
        one_back += [back7, load(base, GROUP - SUBLANES)]
        two_back += [back6, back7, load(base, GROUP - 2 * SUBLANES)]
        roll6, roll7 = new6, new7
    return jnp.concatenate(one_back, axis=0), jnp.concatenate(two_back, axis=0), slab6, slab7


def _gla_start_of_row(x_ref, mem_ref, w_in_ref, wa2_ref, ba_ref, hg_ref, wkv_ref, wo_ref, lng_ref, lnb_ref,
                      out_ref,
                      kbd_ref, vbd_ref, state_ref, q_s, k_s, r_s, la_s, qa_s, rhs_s, kv_s, dc_s, ki_s, ke_s, dec_s,
                      o_s, z_s):
    _build_memory_kv(mem_ref, wkv_ref, kbd_ref, vbd_ref)
    state_ref[...] = jnp.zeros_like(state_ref)
    pad = QA_WIDTH - HEAD_K_PAD - GLA_CHUNK
    for h in range(GLA_HEADS):
        qa_s[:, (h + 1) * QA_WIDTH - pad:(h + 1) * QA_WIDTH] = jnp.zeros((qa_s.shape[0], pad), BF16)
    rhs_s[:, QA_WIDTH - pad:, :] = jnp.zeros((rhs_s.shape[0], pad, HEAD_V_PAD), BF16)
    v_pad = HEAD_V_PAD - GLA_HEAD_V
    rhs_s[:, HEAD_K_PAD:HEAD_K_PAD + GLA_CHUNK, GLA_HEAD_V:] = jnp.zeros((rhs_s.shape[0], GLA_CHUNK, v_pad), BF16)
    for h in range(GLA_HEADS):
        r_s[:, (h + 1) * HEAD_V_PAD - v_pad:(h + 1) * HEAD_V_PAD] = jnp.zeros((r_s.shape[0], v_pad), F32)


def _gla_tile_body(natural_order, x_ref, mem_ref, w_in_ref, wa2_ref, ba_ref, hg_ref, wkv_ref, wo_ref, lng_ref, lnb_ref,
                   out_ref,
                   kbd_ref, vbd_ref, state_ref, q_s, k_s, r_s, la_s, qa_s, rhs_s, kv_s, dc_s, ki_s, ke_s, dec_s,
                      o_s, z_s):
    tile = x_ref.shape[0]
    n_chunks = tile // GLA_CHUNK

    x = x_ref[...]
    xb = x.astype(BF16)
    q_raw = _dot(xb, w_in_ref[:, 0:GLA_COL_K])
    q_s[...] = q_raw * (GLA_HEAD_K ** -0.5)
    z = _dot(q_raw[:, 0:HEAD_K_PAD].astype(BF16), wa2_ref[...]) + ba_ref[...]
    la_s[...] = jax.nn.log_sigmoid(z) * (1.0 / GLA_GATE_TAU)
    k_s[...] = _dot(xb, w_in_ref[:, GLA_COL_K:GLA_COL_V])
    v = _dot(xb, w_in_ref[:, GLA_COL_V:GLA_COL_R])
    for c in range(n_chunks):
        for h in range(GLA_HEADS):
            rhs_s[c * GLA_HEADS + h, HEAD_K_PAD:HEAD_K_PAD + GLA_CHUNK, 0:GLA_HEAD_V] = (
                v[c * GLA_CHUNK:(c + 1) * GLA_CHUNK, h * GLA_HEAD_V:(h + 1) * GLA_HEAD_V].astype(BF16))
    r = _dot(xb, w_in_ref[:, GLA_COL_R:GLA_COL_MQ])
    silu_r = r * jax.nn.sigmoid(r)
    for h in range(GLA_HEADS):
        r_s[:, h * HEAD_V_PAD:h * HEAD_V_PAD + GLA_HEAD_V] = silu_r[:, h * GLA_HEAD_V:(h + 1) * GLA_HEAD_V]

    if natural_order:
        causal = (lax.broadcasted_iota(jnp.int32, (GLA_CHUNK, GLA_CHUNK), 0)
                  >= lax.broadcasted_iota(jnp.int32, (GLA_CHUNK, GLA_CHUNK), 1))
    else:
        causal = (_stored_position((GLA_CHUNK, GLA_CHUNK), 0) >= _stored_position((GLA_CHUNK, GLA_CHUNK), 1))
    tril_ones = causal.astype(BF16)

    for c in range(n_chunks):
        rows = slice(c * GLA_CHUNK, (c + 1) * GLA_CHUNK)
        la = la_s[rows, :]
        la1 = la.astype(BF16)
        rem = la - la1.astype(F32)
        la2 = rem.astype(BF16)
        la3 = (rem - la2.astype(F32)).astype(BF16)
        la_s[rows, :] = _dot(tril_ones, la1) + _dot(tril_ones, la2) + _dot(tril_ones, la3)
    g_cum = la_s[...]
    g_last = [la_s[(c + 1) * GLA_CHUNK - 1:(c + 1) * GLA_CHUNK, :] for c in range(n_chunks)]
    g_last_rows = jnp.concatenate([jnp.broadcast_to(g, (GLA_CHUNK, GLA_DK_PAD)) for g in g_last], axis=0)
    q_dec = (q_s[...] * jnp.exp(g_cum)).astype(BF16)
    for h in range(GLA_HEADS):
        qa_s[:, h * QA_WIDTH:h * QA_WIDTH + HEAD_K_PAD] = q_dec[:, h * HEAD_K_PAD:(h + 1) * HEAD_K_PAD]
    kk = k_s[...]
    ki_s[...] = (kk * jnp.exp(-g_cum)).astype(BF16)
    ke_s[...] = (kk * jnp.exp(g_last_rows - g_cum)).astype(BF16)
    dec_s[...] = jnp.exp(jnp.concatenate(g_last, axis=0))
    for c in range(n_chunks):
        rows = slice(c * GLA_CHUNK, (c + 1) * GLA_CHUNK)
        for h in range(GLA_HEADS):
            att = _dot_general(qa_s[rows, h * QA_WIDTH:h * QA_WIDTH + HEAD_K_PAD],
                               ki_s[rows, h * HEAD_K_PAD:(h + 1) * HEAD_K_PAD], _NT)
            qa_s[rows, h * QA_WIDTH + HEAD_K_PAD:h * QA_WIDTH + HEAD_K_PAD + GLA_CHUNK] = (
                jnp.where(causal, att, 0.0).astype(BF16))
    for c in range(n_chunks):
        rows = slice(c * GLA_CHUNK, (c + 1) * GLA_CHUNK)
        for h in range(GLA_HEADS):
            ks = slice(h * HEAD_K_PAD, (h + 1) * HEAD_K_PAD)
            ch = c * GLA_HEADS + h
            vh = rhs_s[ch, HEAD_K_PAD:HEAD_K_PAD + GLA_CHUNK, :]
            kv_s[ch] = _dot_general(ke_s[rows, ks], vh, _TN)
            dc_s[ch] = jnp.broadcast_to(dec_s[c:c + 1, ks], (HEAD_K_PAD, HEAD_K_PAD)).T

    for c in range(n_chunks):
        for h in range(GLA_HEADS):
            ch = c * GLA_HEADS + h
            st = state_ref[h]
            rhs_s[ch, 0:HEAD_K_PAD, :] = st.astype(BF16)
            dcol = dc_s[ch]
            state_ref[h] = st * jnp.concatenate([dcol, dcol], axis=1) + kv_s[ch]

    for c in range(n_chunks):
        rows = slice(c * GLA_CHUNK, (c + 1) * GLA_CHUNK)
        for h in range(GLA_HEADS):
            vs = slice(h * HEAD_V_PAD, (h + 1) * HEAD_V_PAD)
            o = _dot(qa_s[rows, h * QA_WIDTH:(h + 1) * QA_WIDTH], rhs_s[c * GLA_HEADS + h])
            ms = jnp.sum(o * o, axis=-1, keepdims=True) * (1.0 / GLA_HEAD_V)
            o = o * lax.rsqrt(ms + LN_EPS) * hg_ref[:, vs]
            o_s[rows, h * GLA_HEAD_V:(h + 1) * GLA_HEAD_V] = (o * r_s[rows, vs])[:, 0:GLA_HEAD_V].astype(BF16)

    mem_q = _dot(xb, w_in_ref[:, GLA_COL_MQ:GLA_IN_PAD])
    xa = _memory_xattn(mem_q, kbd_ref, vbd_ref)
    y_mix = _dot(o_s[...], wo_ref[0:GLA_DV, :])
    y_xa = _dot(xa.astype(BF16), wo_ref[GLA_DV:, :])
    _store_slabs(z_s, DEEPNORM_ALPHA * x + (y_mix + y_xa))


def _conv_start_of_row(x_ref, mem_ref, w_in_ref, cw_ref, wkv_ref, wo_ref, lng_ref, lnb_ref,
                       out_ref,
                       kbd_ref, vbd_ref, ch_s, tail_ref, z_s):
    _build_memory_kv(mem_ref, wkv_ref, kbd_ref, vbd_ref)
    tail_ref[...] = jnp.zeros_like(tail_ref)


def _conv_tile_body(x_ref, mem_ref, w_in_ref, cw_ref, wkv_ref, wo_ref, lng_ref, lnb_ref,
                    out_ref,
                    kbd_ref, vbd_ref, ch_s, tail_ref, z_s):
    tile = x_ref.shape[0]
    x = x_ref[...]
    xb = x.astype(BF16)
    c_gate = _dot(xb, w_in_ref[:, MIX_WIDTH:2 * MIX_WIDTH])
    h_in = _dot(xb, w_in_ref[:, 2 * MIX_WIDTH:3 * MIX_WIDTH])
    ch_s[...] = c_gate * h_in
    one_back, two_back, last6, last7 = _causal_taps(
        lambda r0, n: ch_s[r0:r0 + n, :], tile, tail_ref[0:SUBLANES, :], tail_ref[SUBLANES:, :])
    tail_ref[0:SUBLANES, :] = last6
    tail_ref[SUBLANES:, :] = last7
    conv = two_back * cw_ref[0:1, :] + one_back * cw_ref[1:2, :] + ch_s[...] * cw_ref[2:3, :]
    b_gate = _dot(xb, w_in_ref[:, 0:MIX_WIDTH])
    mix = (b_gate * conv).astype(BF16)

    mem_q = _dot(xb, w_in_ref[:, 3 * MIX_WIDTH:])
    xa = _memory_xattn(mem_q, kbd_ref, vbd_ref)
    y_mix = _dot(mix, wo_ref[0:MIX_WIDTH, :])
    y_xa = _dot(xa.astype(BF16), wo_ref[MIX_WIDTH:, :])
    _store_slabs(z_s, DEEPNORM_ALPHA * x + (y_mix + y_xa))


def _ffn_start_of_row(x_ref, wup_ref, cw_ref, cb_ref, wdn_ref, lng_ref, lnb_ref,
                      out_ref,
                      tail_ref, u_s, h_s, z_s):
    tail_ref[...] = jnp.zeros_like(tail_ref)


def _ffn_tile_body(x_ref, wup_ref, cw_ref, cb_ref, wdn_ref, lng_ref, lnb_ref,
                   out_ref,
                   tail_ref, u_s, h_s, z_s):
    tile = x_ref.shape[0]
    x = x_ref[...]
    xb = x.astype(BF16)

    def conv_cols(half, j):
        cols = slice(half * D_FF + j * FFN_CHUNK, half * D_FF + (j + 1) * FFN_CHUNK)
        buf = 2 * (j % 2) + half
        u_s[buf] = _dot(xb, wup_ref[:, cols])
        one_back, two_back, last6, last7 = _causal_taps(
            lambda r0, n: u_s[buf, r0:r0 + n, :], tile, tail_ref[0:SUBLANES, cols], tail_ref[SUBLANES:, cols])
        tail_ref[0:SUBLANES, cols] = last6
        tail_ref[SUBLANES:, cols] = last7
        return (two_back * cw_ref[0:1, cols] + one_back * cw_ref[1:2, cols]
                + u_s[buf] * cw_ref[2:3, cols] + cb_ref[:, cols])

    for j in range(D_FF // FFN_CHUNK):
        gate = conv_cols(0, j)
        val = conv_cols(1, j)
        h_s[:, j * FFN_CHUNK:(j + 1) * FFN_CHUNK] = (gate * jax.nn.sigmoid(gate) * val).astype(BF16)

    _store_slabs(z_s, DEEPNORM_ALPHA * x + _dot(h_s[...], wdn_ref[...]))


def _layer_spec(arr, layer):
    zeros = (0,) * (arr.ndim - 1)
    return pl.BlockSpec((None,) + arr.shape[1:], lambda i: (layer,) + zeros, pipeline_mode=pl.Buffered(1))


def _layer_call(start_of_row, tile_body, name, x, row_inputs, layer_inputs, tile, scratch_shapes,
                swap_order=False):
    bsz, seq, _ = x.shape
    n_tiles = seq // tile
    total = bsz * n_tiles

    def tile_of(i):
        t = jnp.minimum(i, total - 1)
        return t // n_tiles, t % n_tiles

    def x_map(i):
        b, s = tile_of(i)
        return b, s, 0

    def out_map(i):
        b, s = tile_of(jnp.maximum(i - 1, 0))
        return b, s, 0

    def row_map(i):
        return tile_of(i)[0], 0, 0

    in_specs = ([pl.BlockSpec((None, tile, D_MODEL), x_map)]
                + [pl.BlockSpec((None,) + a.shape[1:], row_map) for a in row_inputs]
                + [_layer_spec(a, layer) for a, layer in layer_inputs])
    n_in = len(in_specs)
    return pl.pallas_call(
        functools.partial(_deferred_norm_kernel, start_of_row, tile_body, n_tiles, total, n_in, swap_order),
        grid=(total + 1,),
        in_specs=in_specs,
        out_specs=pl.BlockSpec((None, tile, D_MODEL), out_map),
        out_shape=jax.ShapeDtypeStruct(x.shape, F32),
        scratch_shapes=scratch_shapes + [pltpu.VMEM((D_MODEL // LANES, tile, LANES), F32)],
        compiler_params=pltpu.CompilerParams(dimension_semantics=("arbitrary",),
                                             vmem_limit_bytes=VMEM_LIMIT_BYTES),
        name=name,
    )(x, *row_inputs, *[a for a, _ in layer_inputs])


def _xattn_scratch():
    return [pltpu.VMEM((XA_WIDTH, XA_HEADS * N_MEM), BF16),
            pltpu.VMEM((XA_HEADS * N_MEM, 2 * XA_WIDTH), BF16)]


def _gla_layer(x, mem, layer_inputs, natural_in):
    tile = min(MIX_TILE, x.shape[1])
    n_ch = tile // GLA_CHUNK
    scratch = _xattn_scratch() + [
        pltpu.VMEM((GLA_HEADS, HEAD_K_PAD, HEAD_V_PAD), F32),
        pltpu.VMEM((tile, GLA_DK_PAD), F32),
        pltpu.VMEM((tile, GLA_DK_PAD), F32),
        pltpu.VMEM((tile, GLA_DV_PAD), F32),
        pltpu.VMEM((tile, GLA_DK_PAD), F32),
        pltpu.VMEM((tile, GLA_HEADS * QA_WIDTH), BF16),
        pltpu.VMEM((n_ch * GLA_HEADS, QA_WIDTH, HEAD_V_PAD), BF16),
        pltpu.VMEM((n_ch * GLA_HEADS, HEAD_K_PAD, HEAD_V_PAD), F32),
        pltpu.VMEM((n_ch * GLA_HEADS, HEAD_K_PAD, HEAD_K_PAD), F32),
        pltpu.VMEM((tile, GLA_DK_PAD), BF16),
        pltpu.VMEM((tile, GLA_DK_PAD), BF16),
        pltpu.VMEM((n_ch, GLA_DK_PAD), F32),
        pltpu.VMEM((tile, GLA_DV), BF16),
    ]
    return _layer_call(_gla_start_of_row, functools.partial(_gla_tile_body, natural_in), "gla_layer", x, [mem],
                       layer_inputs, tile, scratch, swap_order=natural_in)


def _conv_layer(x, mem, layer_inputs):
    tile = min(MIX_TILE, x.shape[1])
    scratch = _xattn_scratch() + [
        pltpu.VMEM((tile, MIX_WIDTH), F32),
        pltpu.VMEM((2 * SUBLANES, MIX_WIDTH), F32),
    ]
    return _layer_call(_conv_start_of_row, _conv_tile_body, "conv_layer", x, [mem], layer_inputs, tile, scratch)


def _ffn_layer(x, layer_inputs, natural_out):
    tile = min(FFN_TILE, x.shape[1])
    scratch = [
        pltpu.VMEM((2 * SUBLANES, 2 * D_FF), F32),
        pltpu.VMEM((4, tile, FFN_CHUNK), F32),
        pltpu.VMEM((tile, D_FF), BF16),
    ]
    return _layer_call(_ffn_start_of_row, _ffn_tile_body, "conv_ffn", x, [], layer_inputs, tile, scratch,
                       swap_order=natural_out)


def _pad_heads(w, head_width, padded_width, axis=-1, first_pad=None):
    axis = axis % w.ndim
    pieces = []
    for h in range(GLA_HEADS):
        pieces.append(lax.slice_in_dim(w, h * head_width, (h + 1) * head_width, axis=axis))
        n_zero = padded_width - head_width
        if h == 0 and first_pad is not None:
            pieces.append(first_pad)
            n_zero -= first_pad.shape[axis]
        pieces.append(jnp.zeros(w.shape[:axis] + (n_zero,) + w.shape[axis + 1:], w.dtype))
    return jnp.concatenate(pieces, axis=axis)


def _rows(p):
    return p.reshape(p.shape[0], 1, p.shape[1])


def _pack_gla_weights(gla_w_in, gla_w_a2, gla_b_a, gla_head_g, gla_w_out):
    o_k, o_v, o_r = GLA_DK, 2 * GLA_DK, 2 * GLA_DK + GLA_DV
    o_g = o_r + GLA_DV
    o_mq = o_g + GLA_GATE_RANK
    wq = _pad_heads(gla_w_in[..., :o_k], GLA_HEAD_K, HEAD_K_PAD, first_pad=gla_w_in[..., o_g:o_mq])
    w_in = jnp.concatenate([wq,
                            _pad_heads(gla_w_in[..., o_k:o_v], GLA_HEAD_K, HEAD_K_PAD),
                            gla_w_in[..., o_v:o_g],
                            gla_w_in[..., o_mq:]], axis=-1).astype(BF16)
    a2 = _pad_heads(gla_w_a2, GLA_HEAD_K, HEAD_K_PAD)
    wa2 = jnp.concatenate([jnp.zeros((a2.shape[0], GATE_COL, GLA_DK_PAD), F32), a2,
                           jnp.zeros((a2.shape[0], HEAD_K_PAD - GATE_COL - GLA_GATE_RANK, GLA_DK_PAD), F32)],
                          axis=1).astype(BF16)
    ba = _rows(_pad_heads(gla_b_a, GLA_HEAD_K, HEAD_K_PAD))
    hg = _rows(_pad_heads(gla_head_g, GLA_HEAD_V, HEAD_V_PAD))
    return w_in, wa2, ba, hg, gla_w_out.astype(BF16)


def kernel(x, mem, gla_w_in, gla_w_a2, gla_b_a, gla_head_g, gla_w_out, conv_w_in, conv_w, conv_w_out,
           w_mem_kv, ln1_g, ln1_b, ffn_w_up, ffn_conv_w, ffn_conv_b, ffn_w_down, ln2_g, ln2_b):
    gla_in, gla_a2, gla_ba, gla_hg, gla_wo = _pack_gla_weights(gla_w_in, gla_w_a2, gla_b_a, gla_head_g, gla_w_out)
    conv_in, conv_wo = conv_w_in.astype(BF16), conv_w_out.astype(BF16)
    wkv = w_mem_kv.astype(BF16)
    w_up, w_down = ffn_w_up.astype(BF16), ffn_w_down.astype(BF16)
    ln1g, ln1b, ln2g, ln2b, ffn_cb = _rows(ln1_g), _rows(ln1_b), _rows(ln2_g), _rows(ln2_b), _rows(ffn_conv_b)

    for i in range(DEPTH):
        j = i // 2
        if i % 2 == 0:
            x = _gla_layer(x, mem, [(gla_in, j), (gla_a2, j), (gla_ba, j), (gla_hg, j), (wkv, i), (gla_wo, j),
                                    (ln1g, i), (ln1b, i)], natural_in=(i == 0))
        else:
            x = _conv_layer(x, mem, [(conv_in, j), (conv_w, j), (wkv, i), (conv_wo, j), (ln1g, i), (ln1b, i)])
        x = _ffn_layer(x, [(w_up, i), (ffn_conv_w, i), (ffn_cb, i), (w_down, i), (ln2g, i), (ln2b, i)],
                       natural_out=(i == DEPTH - 1))
    return x
```
